```python
import math
import jax, jax.numpy as jnp
from jax import lax
import numpy as np

D_MODEL = 2048
BATCH = 1
SEQ = 8192
DEPTH = 4

GRID_W = 64
CTX_LEN = 256
HEAD_DIM = 128
N_HEADS = D_MODEL // HEAD_DIM
N_DIFF_HEADS = N_HEADS // 2
N_NA_HEADS = N_HEADS - N_DIFF_HEADS
DIFF_QK_DIM = HEAD_DIM // 2
DIFF_V_DIM = HEAD_DIM
NA_DIM = HEAD_DIM
DIFF_WIDTH = N_DIFF_HEADS * HEAD_DIM
NA_WIDTH = N_NA_HEADS * HEAD_DIM
MIX_WIDTH = DIFF_WIDTH + NA_WIDTH
IN_PROJ_WIDTH = 3 * MIX_WIDTH
WIN_R = 8
WIN_C = 16
Q_BLOCK = 128
ROPE_BASE = 10000.0
N_GROUPS = 4
EXPERTS_PER_GROUP = 8
N_EXPERTS = N_GROUPS * EXPERTS_PER_GROUP
TOP_K_IN_GROUP = 2
EXPERT_FF = D_MODEL // 4
DISPATCH_BLOCK = 128
NORM_EPS = 1e-6
NEG_INF = -1e30

kernel_name = 'hybrid_diffattn_natten_hmoe_dit'


def rms_norm(x, w):
    xf = x.astype(jnp.float32)
    y = xf * lax.rsqrt(jnp.mean(xf * xf, axis=-1, keepdims=True) + NORM_EPS)
    return (y * w.astype(jnp.float32)).astype(x.dtype)


def modulate(h, shift, scale):
    return h * (1 + scale) + shift


def axial_rope(x, prow, pcol):
    half = x.shape[-1] // 2
    nf = half // 2
    inv = ROPE_BASE ** (-jnp.arange(nf, dtype=jnp.float32) / nf)

    def rot(xa, pos):
        ang = pos.astype(jnp.float32)[:, None] * inv[None, :]
        cos = jnp.concatenate([jnp.cos(ang)] * 2, axis=-1)[None, :, None, None, :]
        sin = jnp.concatenate([jnp.sin(ang)] * 2, axis=-1)[None, :, None, None, :]
        x1, x2 = xa[..., :nf], xa[..., nf:]
        return xa * cos + jnp.concatenate([-x2, x1], axis=-1) * sin

    xf = x.astype(jnp.float32)
    out = jnp.concatenate([rot(xf[..., :half], prow), rot(xf[..., half:], pcol)], axis=-1)
    return out.astype(x.dtype)


def split_proj(p):
    B, n = p.shape[:2]
    dq, dk, dv, nq, nk, nv = jnp.split(p, 6, axis=-1)
    dq = dq.reshape(B, n, N_DIFF_HEADS, 2, DIFF_QK_DIM)
    dk = dk.reshape(B, n, N_DIFF_HEADS, 2, DIFF_QK_DIM)
    dv = dv.reshape(B, n, N_DIFF_HEADS, DIFF_V_DIM)
    nq = nq.reshape(B, n, N_NA_HEADS, NA_DIM)
    nk = nk.reshape(B, n, N_NA_HEADS, NA_DIM)
    nv = nv.reshape(B, n, N_NA_HEADS, NA_DIM)
    return dq, dk, dv, nq, nk, nv


def diff_core(q, k, v, lam):
    scale = q.shape[-1] ** -0.5
    s = jnp.einsum('bqhmd,bkhmd->bhmqk', q, k).astype(jnp.float32) * scale
    p = jax.nn.softmax(s, axis=-1)
    a = p[:, :, 0] - lam * p[:, :, 1]
    return jnp.einsum('bhqk,bkhd->bqhd', a.astype(v.dtype), v)


def diff_attention_latent(q_lat, k_all, v_all, lam):
    B, S = q_lat.shape[:2]
    nb = S // Q_BLOCK
    qb = q_lat.reshape(B, nb, Q_BLOCK, N_DIFF_HEADS, 2, DIFF_QK_DIM).swapaxes(0, 1)
    o = lax.map(lambda qblk: diff_core(qblk, k_all, v_all, lam), qb)
    return o.swapaxes(0, 1).reshape(B, S, N_DIFF_HEADS, DIFF_V_DIM)


def dense_attention(q, k, v):
    scale = q.shape[-1] ** -0.5
    s = jnp.einsum('bqhd,bkhd->bhqk', q, k).astype(jnp.float32) * scale
    p = jax.nn.softmax(s, axis=-1).astype(v.dtype)
    return jnp.einsum('bhqk,bkhd->bqhd', p, v)


def neighborhood_attention(q, k, v, k_ctx, v_ctx, rpb):
    B, S, H, Dh = q.shape
    rows = S // GRID_W
    kr = min(WIN_R, rows)
    qg = q.reshape(B, rows, GRID_W, H, Dh)
    kg = k.reshape(B, rows, GRID_W, H, Dh)
    vg = v.reshape(B, rows, GRID_W, H, Dh)
    r = jnp.arange(rows)
    rs = jnp.clip(r - kr // 2, 0, rows - kr)
    krow = rs[:, None] + jnp.arange(kr)[None, :]
    nwin = kr * GRID_W
    kb = kg[:, krow].reshape(B, rows, nwin, H, Dh)
    vb = vg[:, krow].reshape(B, rows, nwin, H, Dh)
    col = jnp.arange(GRID_W)
    cs = jnp.clip(col - WIN_C // 2, 0, GRID_W - WIN_C)
    cmask = (col[None, :] >= cs[:, None]) & (col[None, :] < cs[:, None] + WIN_C)
    mask = jnp.broadcast_to(cmask[:, None, :], (GRID_W, kr, GRID_W)).reshape(GRID_W, nwin)
    dr = krow - r[:, None] + (WIN_R - 1)
    dc = jnp.clip(col[None, :] - col[:, None] + (WIN_C - 1), 0, 2 * WIN_C - 2)
    bias = rpb[:, dr[:, None, :, None], dc[None, :, None, :]]
    bias = bias.reshape(H, rows, GRID_W, nwin).astype(jnp.float32)
    scale = Dh ** -0.5
    s_win = jnp.einsum('brqhd,brkhd->bhrqk', qg, kb).astype(jnp.float32) * scale + bias[None]
    s_win = jnp.where(mask, s_win, NEG_INF)
    s_ctx = jnp.einsum('brqhd,bchd->bhrqc', qg, k_ctx).astype(jnp.float32) * scale
    p = jax.nn.softmax(jnp.concatenate([s_win, s_ctx], axis=-1), axis=-1).astype(v.dtype)
    o = (jnp.einsum('bhrqk,brkhd->brqhd', p[..., :nwin], vb)
         + jnp.einsum('bhrqc,bchd->brqhd', p[..., nwin:], v_ctx))
    return o.reshape(B, S, H, Dh)


def hierarchical_moe(h, w_group, w_expert, w1, w3, w2):
    T = h.shape[0]
    g_logits = (h @ w_group).astype(jnp.float32)
    g_prob = jax.nn.softmax(g_logits, axis=-1)
    g_sel = jnp.argmax(g_logits, axis=-1)
    g_gate = jnp.take_along_axis(g_prob, g_sel[:, None], axis=-1)
    e_logits = (h @ w_expert).astype(jnp.float32).reshape(T, N_GROUPS, EXPERTS_PER_GROUP)
    e_logits = jnp.take_along_axis(e_logits, g_sel[:, None, None], axis=1)[:, 0]
    e_prob = jax.nn.softmax(e_logits, axis=-1)
    top_p, top_i = lax.top_k(e_prob, TOP_K_IN_GROUP)
    top_p = top_p / jnp.sum(top_p, axis=-1, keepdims=True)
    gate = (g_gate * top_p).reshape(-1)
    expert = (g_sel[:, None] * EXPERTS_PER_GROUP + top_i).reshape(-1).astype(jnp.int32)
    token = jnp.repeat(jnp.arange(T, dtype=jnp.int32), TOP_K_IN_GROUP)
    n_assign = T * TOP_K_IN_GROUP
    order = jnp.argsort(expert)
    e_sorted = expert[order]
    tok_sorted = token[order]
    gate_sorted = gate[order]
    counts = jnp.bincount(expert, length=N_EXPERTS)
    starts = jnp.cumsum(counts) - counts
    padded = (counts + DISPATCH_BLOCK - 1) // DISPATCH_BLOCK * DISPATCH_BLOCK
    padded_end = jnp.cumsum(padded)
    padded_start = padded_end - padded
    dest = padded_start[e_sorted] + jnp.arange(n_assign) - starts[e_sorted]
    n_blocks = -(-n_assign // DISPATCH_BLOCK) + N_EXPERTS
    n_slots = n_blocks * DISPATCH_BLOCK
    slot_tok = jnp.zeros((n_slots,), jnp.int32).at[dest].set(tok_sorted)
    slot_gate = jnp.zeros((n_slots,), h.dtype).at[dest].set(gate_sorted.astype(h.dtype))
    block_start = jnp.arange(n_blocks, dtype=padded_end.dtype) * DISPATCH_BLOCK
    block_expert = jnp.clip(jnp.searchsorted(padded_end, block_start, side='right'), 0, N_EXPERTS - 1)

    def expert_block(args):
        toks, e = args
        xb = h[toks]
        return (jax.nn.silu(xb @ w1[e]) * (xb @ w3[e])) @ w2[e]

    y = lax.map(expert_block, (slot_tok.reshape(n_blocks, DISPATCH_BLOCK), block_expert))
    y = y.reshape(n_slots, h.shape[1]) * slot_gate[:, None]
    return jnp.zeros_like(h).at[slot_tok].add(y)


def setup_inputs(seed: int = 0) -> dict:
    key = jax.random.key(seed)
    ks = jax.random.split(key, 25)
    D = D_MODEL

    def nrm(k, shape, std):
        return jax.random.normal(k, shape, jnp.float32) * std

    return {
        'x': nrm(ks[0], (BATCH, SEQ, D), 1.0),
        'c': nrm(ks[1], (BATCH, D), 1.0),
        'ctx': nrm(ks[2], (BATCH, CTX_LEN, D), 1.0),
        'c_ctx': nrm(ks[3], (D,), 1.0),
        'ada_w': nrm(ks[4], (DEPTH, D, 6 * D), 0.5 * D ** -0.5),
        'ada_b': nrm(ks[5], (DEPTH, 6 * D), 0.01),
        'norm1_w': 1.0 + nrm(ks[6], (DEPTH, D), 0.05),
        'norm2_w': 1.0 + nrm(ks[7], (DEPTH, D), 0.05),
        'w_in': nrm(ks[8], (DEPTH, D, IN_PROJ_WIDTH), D ** -0.5),
        'w_out': nrm(ks[9], (DEPTH, MIX_WIDTH, D), MIX_WIDTH ** -0.5),
        'diff_q_norm': 1.0 + nrm(ks[10], (DEPTH, DIFF_QK_DIM), 0.05),
        'diff_k_norm': 1.0 + nrm(ks[11], (DEPTH, DIFF_QK_DIM), 0.05),
        'diff_lq1': nrm(ks[12], (DEPTH, DIFF_QK_DIM), 0.1),
        'diff_lk1': nrm(ks[13], (DEPTH, DIFF_QK_DIM), 0.1),
        'diff_lq2': nrm(ks[14], (DEPTH, DIFF_QK_DIM), 0.1),
        'diff_lk2': nrm(ks[15], (DEPTH, DIFF_QK_DIM), 0.1),
        'diff_subln': 1.0 + nrm(ks[16], (DEPTH, DIFF_V_DIM), 0.05),
        'na_q_norm': 1.0 + nrm(ks[17], (DEPTH, NA_DIM), 0.05),
        'na_k_norm': 1.0 + nrm(ks[18], (DEPTH, NA_DIM), 0.05),
        'na_rpb': nrm(ks[19], (DEPTH, N_NA_HEADS, 2 * WIN_R - 1, 2 * WIN_C - 1), 0.5),
        'moe_w_group': nrm(ks[20], (DEPTH, D, N_GROUPS), D ** -0.5),
        'moe_w_expert': nrm(ks[21], (DEPTH, D, N_EXPERTS), D ** -0.5),
        'moe_w1': nrm(ks[22], (DEPTH, N_EXPERTS, D, EXPERT_FF), D ** -0.5),
        'moe_w3': nrm(ks[23], (DEPTH, N_EXPERTS, D, EXPERT_FF), D ** -0.5),
        'moe_w2': nrm(ks[24], (DEPTH, N_EXPERTS, EXPERT_FF, D), EXPERT_FF ** -0.5),
    }


def reference(x, c, ctx, c_ctx, ada_w, ada_b, norm1_w, norm2_w, w_in, w_out,
              diff_q_norm, diff_k_norm, diff_lq1, diff_lk1, diff_lq2, diff_lk2, diff_subln,
              na_q_norm, na_k_norm, na_rpb,
              moe_w_group, moe_w_expert, moe_w1, moe_w3, moe_w2):
    B, S, D = x.shape
    L = ctx.shape[1]
    pos = jnp.arange(S)
    prow = pos // GRID_W
    pcol = pos % GRID_W
    c_act = jax.nn.silu(c)
    c_ctx_act = jax.nn.silu(c_ctx)[None, :]
    cx = ctx
    for l in range(DEPTH):
        last = l == DEPTH - 1
        mod = (c_act @ ada_w[l] + ada_b[l])[:, None, :]
        mod_c = (c_ctx_act @ ada_w[l] + ada_b[l])[:, None, :]
        sh1, sc1, g1, sh2, sc2, g2 = jnp.split(mod, 6, axis=-1)
        sh1c, sc1c, g1c, sh2c, sc2c, g2c = jnp.split(mod_c, 6, axis=-1)

        h_lat = modulate(rms_norm(x, norm1_w[l]), sh1, sc1)
        h_ctx = modulate(rms_norm(cx, norm1_w[l]), sh1c, sc1c)
        proj = jnp.concatenate([h_ctx, h_lat], axis=1) @ w_in[l]
        dq_c, dk_c, dv_c, nq_c, nk_c, nv_c = split_proj(proj[:, :L])
        dq, dk, dv, nq, nk, nv = split_proj(proj[:, L:])

        dq = axial_rope(rms_norm(dq, diff_q_norm[l]), prow, pcol)
        dk = axial_rope(rms_norm(dk, diff_k_norm[l]), prow, pcol)
        dq_c = rms_norm(dq_c, diff_q_norm[l])
        dk_c = rms_norm(dk_c, diff_k_norm[l])
        lam_init = 0.8 - 0.6 * math.exp(-0.3 * l)
        lam = (jnp.exp(jnp.sum(diff_lq1[l].astype(jnp.float32) * diff_lk1[l].astype(jnp.float32)))
               - jnp.exp(jnp.sum(diff_lq2[l].astype(jnp.float32) * diff_lk2[l].astype(jnp.float32)))
               + lam_init)
        dk_all = jnp.concatenate([dk_c, dk], axis=1)
        dv_all = jnp.concatenate([dv_c, dv], axis=1)
        d_lat = rms_norm(diff_attention_latent(dq, dk_all, dv_all, lam), diff_subln[l]) * (1 - lam_init)

        nq = rms_norm(nq, na_q_norm[l])
        nk = rms_norm(nk, na_k_norm[l])
        nk_c = rms_norm(nk_c, na_k_norm[l])
        n_lat = neighborhood_attention(nq, nk, nv, nk_c, nv_c, na_rpb[l])
        mixed_lat = jnp.concatenate([d_lat.reshape(B, S, DIFF_WIDTH),
                                     n_lat.reshape(B, S, NA_WIDTH)], axis=-1)
        if last:
            x = x + g1 * (mixed_lat @ w_out[l])
            h2 = modulate(rms_norm(x, norm2_w[l]), sh2, sc2)
            y = hierarchical_moe(h2.reshape(B * S, D), moe_w_group[l], moe_w_expert[l],
                                 moe_w1[l], moe_w3[l], moe_w2[l]).reshape(B, S, D)
            x = x + g2 * y
        else:
            d_ctx = rms_norm(diff_core(dq_c, dk_c, dv_c, lam), diff_subln[l]) * (1 - lam_init)
            n_ctx = dense_attention(rms_norm(nq_c, na_q_norm[l]), nk_c, nv_c)
            mixed_ctx = jnp.concatenate([d_ctx.reshape(B, L, DIFF_WIDTH),
                                         n_ctx.reshape(B, L, NA_WIDTH)], axis=-1)
            o = jnp.concatenate([mixed_ctx, mixed_lat], axis=1) @ w_out[l]
            cx = cx + g1c * o[:, :L]
            x = x + g1 * o[:, L:]
            h2 = jnp.concatenate([modulate(rms_norm(cx, norm2_w[l]), sh2c, sc2c),
                                  modulate(rms_norm(x, norm2_w[l]), sh2, sc2)], axis=1)
            y = hierarchical_moe(h2.reshape(B * (L + S), D), moe_w_group[l], moe_w_expert[l],
                                 moe_w1[l], moe_w3[l], moe_w2[l]).reshape(B, L + S, D)
            cx = cx + g2c * y[:, :L]
            x = x + g2 * y[:, L:]
    return x
```

```python
import functools
import math

import numpy as np
import jax
import jax.numpy as jnp
from jax import lax
from jax.experimental import pallas as pl
from jax.experimental.pallas import tpu as pltpu

F32 = jnp.float32
BF16 = jnp.bfloat16

GRID_W = 64
HEAD_DIM = 128
WIN_R = 8
WIN_C = 16
ROPE_BASE = 10000.0
N_GROUPS = 4
EXPERTS_PER_GROUP = 8
N_EXPERTS = N_GROUPS * EXPERTS_PER_GROUP
NORM_EPS = 1e-6
NEG_INF = -1e30
LOG2E = 1.4426950408889634

ROW_TILE = 256
NA_ROWS = ROW_TILE // GRID_W
NA_WIN_TILES = 3
MOE_BLOCK = 256
ADA_TN = 1536
DIFF_TK = 512
VMEM_LIMIT = 56 * 1024 * 1024


def _cparams(sem, vmem=VMEM_LIMIT):
    return pltpu.CompilerParams(dimension_semantics=sem, vmem_limit_bytes=vmem)


def _nt_dot(a, b):
    return lax.dot_general(a, b, (((1,), (1,)), ((), ())), preferred_element_type=F32)


def _adaln_kernel(c_ref, w_ref, b_ref, o_ref):
    c = c_ref[...]
    a = c / (1.0 + jnp.exp(-c))
    o_ref[0] = jnp.dot(a.astype(BF16), w_ref[0].astype(BF16),
                       preferred_element_type=F32) + b_ref[0]


def _adaln(c2, ada_w, ada_b):
    depth, d, six_d = ada_w.shape
    tn = ADA_TN
    return pl.pallas_call(
        _adaln_kernel,
        grid=(depth, six_d // tn),
        in_specs=[pl.BlockSpec((8, d), lambda l, j: (0, 0)),
                  pl.BlockSpec((1, d, tn), lambda l, j: (l, 0, j)),
                  pl.BlockSpec((1, 1, tn), lambda l, j: (l, 0, j))],
        out_specs=pl.BlockSpec((1, 8, tn), lambda l, j: (l, 0, j)),
        out_shape=jax.ShapeDtypeStruct((depth, 8, six_d), F32),
        compiler_params=_cparams(("arbitrary", "arbitrary")),
        name="adaln",
    )(c2, ada_w, ada_b.reshape(depth, 1, six_d))


def _mod_spec(d, chunk):
    return pl.BlockSpec((1, 8, d), lambda i, l: (l[0], 0, chunk))


def _mod_row(ref, is_ctx):
    return jnp.where(is_ctx, ref[0, 1:2, :], ref[0, 0:1, :])


def _norm_mod_kernel(l_ref, x_ref, nw_ref, sh_ref, sc_ref, o_ref):
    is_ctx = pl.program_id(0) == 0
    x = x_ref[...]
    y = x * lax.rsqrt(jnp.mean(x * x, axis=-1, keepdims=True) + NORM_EPS) * nw_ref[0]
    y = y * (1.0 + _mod_row(sc_ref, is_ctx)) + _mod_row(sh_ref, is_ctx)
    o_ref[...] = y.astype(o_ref.dtype)


def _norm_mod(larr, xa, norm_w, mods):
    t, d = xa.shape
    depth = norm_w.shape[0]
    return pl.pallas_call(
        _norm_mod_kernel,
        grid_spec=pltpu.PrefetchScalarGridSpec(
            num_scalar_prefetch=1, grid=(t // ROW_TILE,),
            in_specs=[pl.BlockSpec((ROW_TILE, d), lambda i, l: (i, 0)),
                      pl.BlockSpec((1, 1, d), lambda i, l: (l[0], 0, 0)),
                      _mod_spec(d, 0), _mod_spec(d, 1)],
            out_specs=pl.BlockSpec((ROW_TILE, d), lambda i, l: (i, 0))),
        out_shape=jax.ShapeDtypeStruct((t, d), BF16),
        compiler_params=_cparams(("arbitrary",)),
        name="norm_mod",
    )(larr, xa, norm_w.reshape(depth, 1, d), mods, mods)


def _in_proj_kernel(l_ref, x_ref, w_ref, qkw_ref, cos_ref, sin_ref, o_ref, acc_ref, *, seg):
    j = pl.program_id(0)
    acc_ref[...] = jnp.dot(x_ref[...], w_ref[0], preferred_element_type=F32)
    n_chunks = seg // HEAD_DIM

    @pl.when((j == 2) | (j == 5))
    def _():
        o_ref[...] = acc_ref[...].astype(o_ref.dtype)

    def normed(c, group_mat, inv_n, wrow):
        xc = acc_ref[:, c * HEAD_DIM:(c + 1) * HEAD_DIM]
        ss = jnp.dot((xc * xc).astype(BF16), group_mat, preferred_element_type=F32)
        return xc * lax.rsqrt(ss * inv_n + NORM_EPS) * wrow

    @pl.when(j < 2)
    def _():
        a = lax.broadcasted_iota(jnp.int32, (HEAD_DIM, HEAD_DIM), 0)
        b = lax.broadcasted_iota(jnp.int32, (HEAD_DIM, HEAD_DIM), 1)
        group_mat = ((a & 32) == (b & 32)).astype(BF16)
        wrow = jnp.where(j == 0, qkw_ref[0, 0:1, :], qkw_ref[0, 1:2, :])
        cos = cos_ref[...]
        sin = sin_ref[...]
        for c in range(n_chunks):
            y = normed(c, group_mat, 2.0 / HEAD_DIM, wrow)
            y = y * cos + pltpu.roll(y, HEAD_DIM // 2, 1) * sin
            o_ref[:, c * HEAD_DIM:(c + 1) * HEAD_DIM] = y.astype(o_ref.dtype)

    @pl.when((j == 3) | (j == 4))
    def _():
        group_mat = jnp.ones((HEAD_DIM, HEAD_DIM), BF16)
        wrow = jnp.where(j == 3, qkw_ref[0, 2:3, :], qkw_ref[0, 3:4, :])
        for c in range(n_chunks):
            y = normed(c, group_mat, 1.0 / HEAD_DIM, wrow)
            o_ref[:, c * HEAD_DIM:(c + 1) * HEAD_DIM] = y.astype(o_ref.dtype)


def _pick_tile(n, candidates):
    for c in candidates:
        if n % c == 0:
            return c
    raise ValueError(f"no tile for {n}")


def _in_proj(larr, h, w_in_b, qkw, cos_t, sin_t):
    t, d = h.shape
    seg = w_in_b.shape[2] // 6
    tm = _pick_tile(t, (768, 256))
    return pl.pallas_call(
        functools.partial(_in_proj_kernel, seg=seg),
        grid_spec=pltpu.PrefetchScalarGridSpec(
            num_scalar_prefetch=1, grid=(6, t // tm),
            in_specs=[pl.BlockSpec((tm, d), lambda j, i, l: (i, 0)),
                      pl.BlockSpec((1, d, seg), lambda j, i, l: (l[0], 0, j)),
                      pl.BlockSpec((1, 8, HEAD_DIM), lambda j, i, l: (l[0], 0, 0)),
                      pl.BlockSpec((tm, HEAD_DIM), lambda j, i, l: (i, 0)),
                      pl.BlockSpec((tm, HEAD_DIM), lambda j, i, l: (i, 0))],
            out_specs=pl.BlockSpec((tm, seg), lambda j, i, l: (i, j)),
            scratch_shapes=[pltpu.VMEM((tm, seg), F32)]),
        out_shape=jax.ShapeDtypeStruct((t, 6 * seg), BF16),
        compiler_params=_cparams(("arbitrary", "arbitrary")),
        name="in_proj",
    )(larr, h, w_in_b, qkw, cos_t, sin_t)


def _diff_attn_kernel(l_ref, lam_ref, q_ref, k_ref, v_ref, sw_ref, o_ref, m_ref, s_ref, acc_ref,
                      *, ctx_len, tk):
    i = pl.program_id(1)
    layer = l_ref[0]
    t_all = k_ref.shape[0]
    lane = lax.broadcasted_iota(jnp.int32, (1, HEAD_DIM), 1)
    in_map0 = (lane & 32) == 0
    q = q_ref[...]
    qs = (jnp.where(in_map0, q, jnp.zeros_like(q)), jnp.where(in_map0, jnp.zeros_like(q), q))

    m_ref[...] = jnp.full(m_ref.shape, -jnp.inf, F32)
    s_ref[...] = jnp.zeros(s_ref.shape, F32)
    acc_ref[...] = jnp.zeros(acc_ref.shape, F32)

    def step(kc, vc):
        for mi in range(2):
            s = _nt_dot(qs[mi], kc)
            m_prev = m_ref[mi]
            m_new = jnp.maximum(m_prev, jnp.max(s, axis=-1, keepdims=True))
            alpha = jnp.exp2(m_prev - m_new)
            p = jnp.exp2(s - m_new)
            s_ref[mi] = alpha * s_ref[mi] + jnp.sum(p, axis=-1, keepdims=True)
            acc_ref[mi] = alpha * acc_ref[mi] + jnp.dot(p.astype(BF16), vc,
                                                       preferred_element_type=F32)
            m_ref[mi] = m_new

    step(k_ref[0:ctx_len, :], v_ref[0:ctx_len, :])

    def body(c, carry):
        off = pl.multiple_of(ctx_len + c * tk, tk if ctx_len % tk == 0 else ROW_TILE)
        step(k_ref[pl.ds(off, tk), :], v_ref[pl.ds(off, tk), :])
        return carry

    n_lat = jnp.where(i == 0, 0, (t_all - ctx_len) // tk)
    lax.fori_loop(0, n_lat, body, 0)

    lam = lam_ref[2 * layer]
    out_scale = lam_ref[2 * layer + 1]
    o = acc_ref[0] / s_ref[0] - lam * (acc_ref[1] / s_ref[1])
    o = o * lax.rsqrt(jnp.mean(o * o, axis=-1, keepdims=True) + NORM_EPS) * sw_ref[0] * out_scale
    o_ref[...] = o.astype(o_ref.dtype)


def _diff_attn(larr, lam_tab, proj, subln, n_heads, ctx_len):
    t = proj.shape[0]
    depth = subln.shape[0]
    tq = ROW_TILE
    tk = _pick_tile(t - ctx_len, (DIFF_TK, 256))
    return pl.pallas_call(
        functools.partial(_diff_attn_kernel, ctx_len=ctx_len, tk=tk),
        grid_spec=pltpu.PrefetchScalarGridSpec(
            num_scalar_prefetch=1, grid=(n_heads, t // tq),
            in_specs=[pl.BlockSpec(memory_space=pltpu.SMEM),
                      pl.BlockSpec((tq, HEAD_DIM), lambda h, i, l: (i, h)),
                      pl.BlockSpec((t, HEAD_DIM), lambda h, i, l: (0, n_heads + h)),
                      pl.BlockSpec((t, HEAD_DIM), lambda h, i, l: (0, 2 * n_heads + h)),
                      pl.BlockSpec((1, 1, HEAD_DIM), lambda h, i, l: (l[0], 0, 0))],
            out_specs=pl.BlockSpec((tq, HEAD_DIM), lambda h, i, l: (i, h)),
            scratch_shapes=[pltpu.VMEM((2, tq, 1), F32), pltpu.VMEM((2, tq, 1), F32),
                            pltpu.VMEM((2, tq, HEAD_DIM), F32)]),
        out_shape=jax.ShapeDtypeStruct((t, n_heads * HEAD_DIM), BF16),
        compiler_params=_cparams(("arbitrary", "arbitrary")),
        name="diff_attn",
    )(larr, lam_tab, proj, proj, proj, subln.reshape(depth, 1, HEAD_DIM))


def _na_kernel(l_ref, q_ref, kc_ref, k0_ref, k1_ref, k2_ref, vc_ref, v0_ref, v1_ref, v2_ref,
               bias_ref, o_ref, *, n_heads):
    kw = (k0_ref, k1_ref, k2_ref)
    vw = (v0_ref, v1_ref, v2_ref)
    for h in range(n_heads):
        hs = slice(h * HEAD_DIM, (h + 1) * HEAD_DIM)
        qh = q_ref[:, hs]
        s_c = _nt_dot(qh, kc_ref[:, hs])
        s_w = [_nt_dot(qh, kw[j][:, hs]) + bias_ref[0, 0, h, :, j * ROW_TILE:(j + 1) * ROW_TILE]
               for j in range(NA_WIN_TILES)]
        m = jnp.max(s_c, axis=-1, keepdims=True)
        for s in s_w:
            m = jnp.maximum(m, jnp.max(s, axis=-1, keepdims=True))
        p_c = jnp.exp2(s_c - m)
        denom = jnp.sum(p_c, axis=-1, keepdims=True)
        o = jnp.dot(p_c.astype(BF16), vc_ref[:, hs], preferred_element_type=F32)
        for j in range(NA_WIN_TILES):
            p = jnp.exp2(s_w[j] - m)
            denom = denom + jnp.sum(p, axis=-1, keepdims=True)
            o = o + jnp.dot(p.astype(BF16), vw[j][:, hs], preferred_element_type=F32)
        o_ref[:, hs] = (o / denom).astype(o_ref.dtype)


def _na_win_base(g, n_tiles):
    return 1 + jnp.clip(g - 2, 0, n_tiles - 1 - NA_WIN_TILES)


def _na_pattern(g, n_tiles):
    return jnp.where(g == 0, 0, jnp.where(g == 1, 1, jnp.where(g == n_tiles - 1, 3, 2)))


def _na_attn(larr, proj, bias_tbl, n_heads, n_diff_heads):
    t = proj.shape[0]
    n_tiles = t // ROW_TILE
    w = n_heads * HEAD_DIM
    qcol = 3 * n_diff_heads * HEAD_DIM // w
    blk = (ROW_TILE, w)

    def win_spec(col, j):
        return pl.BlockSpec(blk, lambda g, l: (_na_win_base(g, n_tiles) + j, col))

    in_specs = [pl.BlockSpec(blk, lambda g, l: (g, qcol)),
                pl.BlockSpec(blk, lambda g, l: (0, qcol + 1))]
    in_specs += [win_spec(qcol + 1, j) for j in range(NA_WIN_TILES)]
    in_specs += [pl.BlockSpec(blk, lambda g, l: (0, qcol + 2))]
    in_specs += [win_spec(qcol + 2, j) for j in range(NA_WIN_TILES)]
    in_specs += [pl.BlockSpec((1, 1, n_heads, ROW_TILE, NA_WIN_TILES * ROW_TILE),
                              lambda g, l: (l[0], _na_pattern(g, n_tiles), 0, 0, 0))]
    return pl.pallas_call(
        functools.partial(_na_kernel, n_heads=n_heads),
        grid_spec=pltpu.PrefetchScalarGridSpec(
            num_scalar_prefetch=1, grid=(n_tiles,),
            in_specs=in_specs,
            out_specs=pl.BlockSpec(blk, lambda g, l: (g, 0))),
        out_shape=jax.ShapeDtypeStruct((t, w), BF16),
        compiler_params=_cparams(("arbitrary",)),
        name="na_attn",
    )(larr, *([proj] * 9), bias_tbl)


def _na_bias_table(rpb, n_tiles):
    rows = (n_tiles - 1) * NA_ROWS
    kr = min(WIN_R, rows)
    pats = []
    for gl in (0, 1, n_tiles - 2):
        qi = np.arange(ROW_TILE)
        kj = np.arange(NA_WIN_TILES * ROW_TILE)
        qr = NA_ROWS * gl + qi // GRID_W
        qc = qi % GRID_W
        win_start = NA_ROWS * int(np.clip(gl - 1, 0, n_tiles - 1 - NA_WIN_TILES))
        krow = win_start + kj // GRID_W
        kc = kj % GRID_W
        rs = np.clip(qr - kr // 2, 0, rows - kr)
        cs = np.clip(qc - WIN_C // 2, 0, GRID_W - WIN_C)
        valid = ((krow[None, :] >= rs[:, None]) & (krow[None, :] < rs[:, None] + kr)
                 & (kc[None, :] >= cs[:, None]) & (kc[None, :] < cs[:, None] + WIN_C))
        dr = np.clip(krow[None, :] - qr[:, None] + (WIN_R - 1), 0, 2 * WIN_R - 2)
        dc = np.clip(kc[None, :] - qc[:, None] + (WIN_C - 1), 0, 2 * WIN_C - 2)
        pats.append((valid, dr, dc))
    valid = np.stack([p[0] for p in pats])
    dr = np.stack([p[1] for p in pats])
    dc = np.stack([p[2] for p in pats])
    b = rpb[:, :, dr, dc]
    b = jnp.where(valid[None, None], b, NEG_INF) * LOG2E
    b = jnp.swapaxes(b, 1, 2)
    masked = jnp.full_like(b[:, :1], NEG_INF * LOG2E)
    return jnp.concatenate([masked, b], axis=1)


def _out_router_kernel(l_ref, d_ref, n_ref, wa_ref, wb_ref, x_ref, g1_ref, nw_ref, sh_ref, sc_ref,
                       wr_ref, xo_ref, h_ref, ids_ref, gates_ref):
    is_ctx = pl.program_id(0) == 0
    acc = jnp.dot(d_ref[...], wa_ref[0], preferred_element_type=F32)
    acc = acc + jnp.dot(n_ref[...], wb_ref[0], preferred_element_type=F32)
    x = x_ref[...] + _mod_row(g1_ref, is_ctx) * acc
    xo_ref[...] = x
    y = x * lax.rsqrt(jnp.mean(x * x, axis=-1, keepdims=True) + NORM_EPS) * nw_ref[0]
    y = y * (1.0 + _mod_row(sc_ref, is_ctx)) + _mod_row(sh_ref, is_ctx)
    h_ref[...] = y

    y_hi = y.astype(BF16)
    y_lo = (y - y_hi.astype(F32)).astype(BF16)
    lg = jnp.dot(y_hi, wr_ref[0, 0], preferred_element_type=F32)
    lg = lg + (jnp.dot(y_lo, wr_ref[0, 0], preferred_element_type=F32)
               + jnp.dot(y_hi, wr_ref[0, 1], preferred_element_type=F32))

    lane = lax.broadcasted_iota(jnp.int32, lg.shape, 1).astype(F32)
    big = jnp.float32(1e9)
    is_g = (lane >= N_EXPERTS) & (lane < N_EXPERTS + N_GROUPS)
    gl = jnp.where(is_g, lg, -jnp.inf)
    gmax = jnp.max(gl, axis=-1, keepdims=True)
    gsel = jnp.min(jnp.where(gl == gmax, lane, big), axis=-1, keepdims=True) - N_EXPERTS
    g_gate = 1.0 / jnp.sum(jnp.where(is_g, jnp.exp(gl - gmax), 0.0), axis=-1, keepdims=True)
    lo = gsel * EXPERTS_PER_GROUP
    el = jnp.where((lane >= lo) & (lane < lo + EXPERTS_PER_GROUP), lg, -jnp.inf)
    e1 = jnp.max(el, axis=-1, keepdims=True)
    i1 = jnp.min(jnp.where(el == e1, lane, big), axis=-1, keepdims=True)
    el2 = jnp.where(lane == i1, -jnp.inf, el)
    e2 = jnp.max(el2, axis=-1, keepdims=True)
    i2 = jnp.min(jnp.where(el2 == e2, lane, big), axis=-1, keepdims=True)
    r = jnp.exp(e2 - e1)
    w1 = 1.0 / (1.0 + r)
    w2 = r / (1.0 + r)
    ids_ref[...] = jnp.where(lane == 0, i1, jnp.where(lane == 1, i2, 0.0)).astype(jnp.int32)
    gates_ref[...] = jnp.where(lane == 0, g_gate * w1, jnp.where(lane == 1, g_gate * w2, 0.0))


def _out_router(larr, d_out, n_out, w_out_b, xa, mods, norm2_w, wr):
    t, d = xa.shape
    half = d_out.shape[1]
    depth = norm2_w.shape[0]
    tile = lambda width: pl.BlockSpec((ROW_TILE, width), lambda i, l: (i, 0))
    return pl.pallas_call(
        _out_router_kernel,
        grid_spec=pltpu.PrefetchScalarGridSpec(
            num_scalar_prefetch=1, grid=(t // ROW_TILE,),
            in_specs=[tile(half), tile(half),
                      pl.BlockSpec((1, half, d), lambda i, l: (l[0], 0, 0)),
                      pl.BlockSpec((1, half, d), lambda i, l: (l[0], 1, 0)),
                      tile(d), _mod_spec(d, 2),
                      pl.BlockSpec((1, 1, d), lambda i, l: (l[0], 0, 0)),
                      _mod_spec(d, 3), _mod_spec(d, 4),
                      pl.BlockSpec((1, 2, d, HEAD_DIM), lambda i, l: (l[0], 0, 0, 0))],
            out_specs=[tile(d), tile(d), tile(HEAD_DIM), tile(HEAD_DIM)]),
        out_shape=[jax.ShapeDtypeStruct((t, d), F32), jax.ShapeDtypeStruct((t, d), F32),
                   jax.ShapeDtypeStruct((t, HEAD_DIM), jnp.int32),
                   jax.ShapeDtypeStruct((t, HEAD_DIM), F32)],
        compiler_params=_cparams(("arbitrary",)),
        name="out_router",
    )(larr, d_out, n_out, w_out_b, w_out_b, xa, mods, norm2_w.reshape(depth, 1, d), mods, mods, wr)


def _expert_kernel(l_ref, be_ref, st_ref, nu_ref, h_hbm, w1_ref, w3_ref, w2_ref, y_ref,
                   xbuf, wb1, wb3, wb2, sem):
    b = pl.program_id(0)

    @pl.when(b < nu_ref[0])
    def _():
        def issue(r, carry):
            tok = st_ref[b * MOE_BLOCK + r]
            pltpu.make_async_copy(h_hbm.at[pl.ds(tok, 1), :], xbuf.at[pl.ds(r, 1), :], sem).start()
            return carry
        lax.fori_loop(0, MOE_BLOCK, issue, 0)

        @pl.when((b == 0) | (be_ref[b] != be_ref[jnp.maximum(b - 1, 0)]))
        def _():
            wb1[...] = w1_ref[0, 0].astype(BF16)
            wb3[...] = w3_ref[0, 0].astype(BF16)
            wb2[...] = w2_ref[0, 0].astype(BF16)

        pltpu.make_async_copy(h_hbm.at[pl.ds(0, MOE_BLOCK), :], xbuf, sem).wait()
        xb = xbuf[...].astype(BF16)
        a = jnp.dot(xb, wb1[...], preferred_element_type=F32)
        g = jnp.dot(xb, wb3[...], preferred_element_type=F32)
        mid = (a / (1.0 + jnp.exp(-a))) * g
        y_ref[...] = jnp.dot(mid.astype(BF16), wb2[...], preferred_element_type=F32)

    @pl.when(b >= nu_ref[0])
    def _():
        y_ref[...] = jnp.zeros(y_ref.shape, y_ref.dtype)


def _experts(larr, block_expert, slot_tok, n_used, h2, w1, w3, w2):
    t, d = h2.shape
    ff = w1.shape[-1]
    n_blocks = block_expert.shape[0]
    return pl.pallas_call(
        _expert_kernel,
        grid_spec=pltpu.PrefetchScalarGridSpec(
            num_scalar_prefetch=4, grid=(n_blocks,),
            in_specs=[pl.BlockSpec(memory_space=pl.ANY),
                      pl.BlockSpec((1, 1, d, ff), lambda b, l, be, st, nu: (l[0], be[b], 0, 0)),
                      pl.BlockSpec((1, 1, d, ff), lambda b, l, be, st, nu: (l[0], be[b], 0, 0)),
                      pl.BlockSpec((1, 1, ff, d), lambda b, l, be, st, nu: (l[0], be[b], 0, 0))],
            out_specs=pl.BlockSpec((MOE_BLOCK, d), lambda b, l, be, st, nu: (b, 0)),
            scratch_shapes=[pltpu.VMEM((MOE_BLOCK, d), F32),
                            pltpu.VMEM((d, ff), BF16), pltpu.VMEM((d, ff), BF16),
                            pltpu.VMEM((ff, d), BF16),
                            pltpu.SemaphoreType.DMA(())]),
        out_shape=jax.ShapeDtypeStruct((n_blocks * MOE_BLOCK, d), F32),
        compiler_params=_cparams(("arbitrary",)),
        name="experts",
    )(larr, block_expert, slot_tok, n_used, h2, w1, w3, w2)


def _combine_kernel(l_ref, dest_ref, y_hbm, x_ref, gates_ref, g2_ref, o_ref, ybuf, sem):
    i = pl.program_id(0)
    is_ctx = i == 0

    def issue(r, carry):
        for k in range(2):
            slot = dest_ref[(i * ROW_TILE + r) * 2 + k]
            pltpu.make_async_copy(y_hbm.at[pl.ds(slot, 1), :],
                                  ybuf.at[pl.ds(k * ROW_TILE + r, 1), :], sem).start()
        return carry
    lax.fori_loop(0, ROW_TILE, issue, 0)
    pltpu.make_async_copy(y_hbm.at[pl.ds(0, 2 * ROW_TILE), :], ybuf, sem).wait()
    gates = gates_ref[...]
    y = gates[:, 0:1] * ybuf[0:ROW_TILE, :] + gates[:, 1:2] * ybuf[ROW_TILE:2 * ROW_TILE, :]
    o_ref[...] = x_ref[...] + _mod_row(g2_ref, is_ctx) * y


def _combine(larr, dest, y_slots, x_new, gates, mods):
    t, d = x_new.shape
    return pl.pallas_call(
        _combine_kernel,
        grid_spec=pltpu.PrefetchScalarGridSpec(
            num_scalar_prefetch=2, grid=(t // ROW_TILE,),
            in_specs=[pl.BlockSpec(memory_space=pl.ANY),
                      pl.BlockSpec((ROW_TILE, d), lambda i, l, dd: (i, 0)),
                      pl.BlockSpec((ROW_TILE, HEAD_DIM), lambda i, l, dd: (i, 0)),
                      pl.BlockSpec((1, 8, d), lambda i, l, dd: (l[0], 0, 5))],
            out_specs=pl.BlockSpec((ROW_TILE, d), lambda i, l, dd: (i, 0)),
            scratch_shapes=[pltpu.VMEM((2 * ROW_TILE, d), F32),
                            pltpu.SemaphoreType.DMA(())]),
        out_shape=jax.ShapeDtypeStruct((t, d), F32),
        compiler_params=_cparams(("arbitrary",)),
        name="combine",
    )(larr, dest, y_slots, x_new, gates, mods)


def _dispatch(ids):
    t = ids.shape[0]
    n_assign = 2 * t
    e_flat = ids.reshape(n_assign)
    onehot = (e_flat[:, None] == jnp.arange(N_EXPERTS, dtype=jnp.int32)[None, :]).astype(jnp.int32)
    csum = jnp.cumsum(onehot, axis=0)
    rank = jnp.sum(onehot * (csum - 1), axis=1)
    counts = csum[-1]
    padded = (counts + MOE_BLOCK - 1) // MOE_BLOCK * MOE_BLOCK
    padded_end = jnp.cumsum(padded)
    padded_start = padded_end - padded
    dest = (padded_start[e_flat] + rank).astype(jnp.int32)
    n_blocks = -(-n_assign // MOE_BLOCK) + N_EXPERTS
    tok = jnp.arange(n_assign, dtype=jnp.int32) // 2
    slot_tok = jnp.zeros((n_blocks * MOE_BLOCK,), jnp.int32).at[dest].set(tok)
    block_start = jnp.arange(n_blocks, dtype=jnp.int32) * MOE_BLOCK
    block_expert = jnp.clip(jnp.searchsorted(padded_end, block_start, side='right'),
                            0, N_EXPERTS - 1).astype(jnp.int32)
    n_used = (padded_end[-1:] // MOE_BLOCK).astype(jnp.int32)
    return dest, slot_tok, block_expert, n_used


def _diff_lane_fields():
    lane = np.arange(HEAD_DIM)
    part, m, half, f = lane // 64, (lane % 64) // 32, (lane % 32) // 16, lane % 16
    return part, m, half, f


def _rope_tables(ctx_len, seq):
    part, m, half, f = _diff_lane_fields()
    nf = HEAD_DIM // 8
    inv = ROPE_BASE ** (-jnp.arange(nf, dtype=F32) / nf)
    pos = jnp.arange(seq)
    prow = (pos // GRID_W).astype(F32)
    pcol = (pos % GRID_W).astype(F32)
    p = jnp.where(jnp.asarray(half)[None, :] == 0, prow[:, None], pcol[:, None])
    ang = p * inv[jnp.asarray(f)][None, :]
    sign = jnp.where(jnp.asarray(part) == 0, -1.0, 1.0).astype(F32)[None, :]
    cos = jnp.concatenate([jnp.ones((ctx_len, HEAD_DIM), F32), jnp.cos(ang)], axis=0)
    sin = jnp.concatenate([jnp.zeros((ctx_len, HEAD_DIM), F32), jnp.sin(ang) * sign], axis=0)
    return cos, sin


def _permute_diff_cols(w, n_heads):
    lead = w.shape[:-1]
    w = w.reshape(lead + (n_heads, 2, 2, 2, 16))
    nd = len(lead)
    w = jnp.transpose(w, tuple(range(nd)) + (nd, nd + 3, nd + 1, nd + 2, nd + 4))
    return w.reshape(lead + (n_heads * HEAD_DIM,))


def _qk_weight_table(diff_q_norm, diff_k_norm, na_q_norm, na_k_norm):
    part, m, half, f = _diff_lane_fields()
    src = half * 32 + part * 16 + f
    dqk = HEAD_DIM // 2
    dq = diff_q_norm[:, src] * (dqk ** -0.5 * LOG2E)
    dk = diff_k_norm[:, src]
    nq = na_q_norm * (HEAD_DIM ** -0.5 * LOG2E)
    rows = jnp.stack([dq, dk, nq, na_k_norm], axis=1)
    return jnp.concatenate([rows, jnp.zeros_like(rows)], axis=1)


def kernel(x, c, ctx, c_ctx, ada_w, ada_b, norm1_w, norm2_w, w_in, w_out, diff_q_norm, diff_k_norm,
           diff_lq1, diff_lk1, diff_lq2, diff_lk2, diff_subln, na_q_norm, na_k_norm, na_rpb,
           moe_w_group, moe_w_expert, moe_w1, moe_w3, moe_w2):
    _, seq, d = x.shape
    ctx_len = ctx.shape[1]
    depth = ada_w.shape[0]
    assert ctx_len == ROW_TILE and seq % ROW_TILE == 0 and x.shape[0] == 1
    n_heads = d // HEAD_DIM
    n_diff = n_heads // 2
    n_na = n_heads - n_diff
    seg = n_diff * HEAD_DIM
    t = ctx_len + seq
    n_tiles = t // ROW_TILE

    xa = jnp.concatenate([ctx[0], x[0]], axis=0)
    c2 = jnp.zeros((8, d), F32).at[0].set(c[0]).at[1].set(c_ctx)
    mods = _adaln(c2, ada_w, ada_b)

    w_in_b = jnp.concatenate([_permute_diff_cols(w_in[..., :seg], n_diff),
                              _permute_diff_cols(w_in[..., seg:2 * seg], n_diff),
                              w_in[..., 2 * seg:]], axis=-1).astype(BF16)
    w_out_b = w_out.astype(BF16)
    qkw = _qk_weight_table(diff_q_norm, diff_k_norm, na_q_norm, na_k_norm)
    cos_t, sin_t = _rope_tables(ctx_len, seq)
    bias_tbl = _na_bias_table(na_rpb, n_tiles)

    lam_init = jnp.asarray([0.8 - 0.6 * math.exp(-0.3 * l) for l in range(depth)], F32)
    lam = (jnp.exp(jnp.sum(diff_lq1 * diff_lk1, axis=-1)) - jnp.exp(jnp.sum(diff_lq2 * diff_lk2, axis=-1))
           + lam_init)
    lam_tab = jnp.stack([lam, 1.0 - lam_init], axis=1).reshape(2 * depth)

    wr = jnp.concatenate([moe_w_expert, moe_w_group,
                          jnp.zeros((depth, d, HEAD_DIM - N_EXPERTS - N_GROUPS), F32)], axis=-1)
    wr_hi = wr.astype(BF16)
    wr_lo = (wr - wr_hi.astype(F32)).astype(BF16)
    wr_split = jnp.stack([wr_hi, wr_lo], axis=1)

    for l in range(depth):
        larr = jnp.full((1,), l, jnp.int32)
        h = _norm_mod(larr, xa, norm1_w, mods)
        proj = _in_proj(larr, h, w_in_b, qkw, cos_t, sin_t)
        d_out = _diff_attn(larr, lam_tab, proj, diff_subln, n_diff, ctx_len)
        n_out = _na_attn(larr, proj, bias_tbl, n_na, n_diff)
        x_new, h2, ids, gates = _out_router(larr, d_out, n_out, w_out_b, xa, mods, norm2_w, wr_split)
        dest, slot_tok, block_expert, n_used = _dispatch(ids[:, :2])
        y_slots = _experts(larr, block_expert, slot_tok, n_used, h2, moe_w1, moe_w3, moe_w2)
        xa = _combine(larr, dest, y_slots, x_new, gates, mods)
    return xa[ctx_len:][None]
```

```python
import functools
import math

import numpy as np
import jax
import jax.numpy as jnp
from jax import lax
from jax.experimental import pallas as pl
from jax.experimental.pallas import tpu as pltpu

F32 = jnp.float32
BF16 = jnp.bfloat16

GRID_W = 64
HEAD_DIM = 128
WIN_R = 8
WIN_C = 16
ROPE_BASE = 10000.0
N_GROUPS = 4
EXPERTS_PER_GROUP = 8
N_EXPERTS = N_GROUPS * EXPERTS_PER_GROUP
NORM_EPS = 1e-6
NEG_INF = -1e30
LOG2E = 1.4426950408889634

ROW_TILE = 256
NA_ROWS = ROW_TILE // GRID_W
NA_WIN_TILES = 3
MOE_BLOCK = 256
ADA_TN = 1536
DIFF_TK = 512
VMEM_LIMIT = 56 * 1024 * 1024


def _cparams(sem, vmem=VMEM_LIMIT):
    return pltpu.CompilerParams(dimension_semantics=sem, vmem_limit_bytes=vmem)


def _nt_dot(a, b):
    return lax.dot_general(a, b, (((1,), (1,)), ((), ())), preferred_element_type=F32)


def _adaln_kernel(c_ref, w_ref, b_ref, o_ref):
    c = c_ref[...]
    a = c / (1.0 + jnp.exp(-c))
    o_ref[0] = jnp.dot(a.astype(BF16), w_ref[0].astype(BF16),
                       preferred_element_type=F32) + b_ref[0]


def _adaln(c2, ada_w, ada_b):
    depth, d, six_d = ada_w.shape
    tn = ADA_TN
    return pl.pallas_call(
        _adaln_kernel,
        grid=(depth, six_d // tn),
        in_specs=[pl.BlockSpec((8, d), lambda l, j: (0, 0)),
                  pl.BlockSpec((1, d, tn), lambda l, j: (l, 0, j)),
                  pl.BlockSpec((1, 1, tn), lambda l, j: (l, 0, j))],
        out_specs=pl.BlockSpec((1, 8, tn), lambda l, j: (l, 0, j)),
        out_shape=jax.ShapeDtypeStruct((depth, 8, six_d), F32),
        compiler_params=_cparams(("arbitrary", "arbitrary")),
        name="adaln",
    )(c2, ada_w, ada_b.reshape(depth, 1, six_d))


def _mod_spec(d, chunk):
    return pl.BlockSpec((1, 8, d), lambda i, l: (l[0], 0, chunk))


def _mod_row(ref, is_ctx):
    return jnp.where(is_ctx, ref[0, 1:2, :], ref[0, 0:1, :])


def _norm_mod_kernel(l_ref, x_ref, nw_ref, sh_ref, sc_ref, o_ref):
    is_ctx = pl.program_id(0) == 0
    x = x_ref[...]
    y = x * lax.rsqrt(jnp.mean(x * x, axis=-1, keepdims=True) + NORM_EPS) * nw_ref[0]
    y = y * (1.0 + _mod_row(sc_ref, is_ctx)) + _mod_row(sh_ref, is_ctx)
    o_ref[...] = y.astype(o_ref.dtype)


def _norm_mod(larr, xa, norm_w, mods):
    t, d = xa.shape
    depth = norm_w.shape[0]
    return pl.pallas_call(
        _norm_mod_kernel,
        grid_spec=pltpu.PrefetchScalarGridSpec(
            num_scalar_prefetch=1, grid=(t // ROW_TILE,),
            in_specs=[pl.BlockSpec((ROW_TILE, d), lambda i, l: (i, 0)),
                      pl.BlockSpec((1, 1, d), lambda i, l: (l[0], 0, 0)),
                      _mod_spec(d, 0), _mod_spec(d, 1)],
            out_specs=pl.BlockSpec((ROW_TILE, d), lambda i, l: (i, 0))),
        out_shape=jax.ShapeDtypeStruct((t, d), BF16),
        compiler_params=_cparams(("arbitrary",)),
        name="norm_mod",
    )(larr, xa, norm_w.reshape(depth, 1, d), mods, mods)


def _in_proj_kernel(l_ref, x_ref, w_ref, qkw_ref, cos_ref, sin_ref, o_ref, acc_ref, *, seg):
    j = pl.program_id(0)
    acc_ref[...] = jnp.dot(x_ref[...], w_ref[0], preferred_element_type=F32)
    n_chunks = seg // HEAD_DIM

    @pl.when((j == 2) | (j == 5))
    def _():
        o_ref[...] = acc_ref[...].astype(o_ref.dtype)

    def normed(c, group_mat, inv_n, wrow):
        xc = acc_ref[:, c * HEAD_DIM:(c + 1) * HEAD_DIM]
        ss = jnp.dot((xc * xc).astype(BF16), group_mat, preferred_element_type=F32)
        return xc * lax.rsqrt(ss * inv_n + NORM_EPS) * wrow

    @pl.when(j < 2)
    def _():
        a = lax.broadcasted_iota(jnp.int32, (HEAD_DIM, HEAD_DIM), 0)
        b = lax.broadcasted_iota(jnp.int32, (HEAD_DIM, HEAD_DIM), 1)
        group_mat = ((a & 32) == (b & 32)).astype(BF16)
        wrow = jnp.where(j == 0, qkw_ref[0, 0:1, :], qkw_ref[0, 1:2, :])
        cos = cos_ref[...]
        sin = sin_ref[...]
        for c in range(n_chunks):
            y = normed(c, group_mat, 2.0 / HEAD_DIM, wrow)
            y = y * cos + pltpu.roll(y, HEAD_DIM // 2, 1) * sin
            o_ref[:, c * HEAD_DIM:(c + 1) * HEAD_DIM] = y.astype(o_ref.dtype)

    @pl.when((j == 3) | (j == 4))
    def _():
        group_mat = jnp.ones((HEAD_DIM, HEAD_DIM), BF16)
        wrow = jnp.where(j == 3, qkw_ref[0, 2:3, :], qkw_ref[0, 3:4, :])
        for c in range(n_chunks):
            y = normed(c, group_mat, 1.0 / HEAD_DIM, wrow)
            o_ref[:, c * HEAD_DIM:(c + 1) * HEAD_DIM] = y.astype(o_ref.dtype)


def _pick_tile(n, candidates):
    for c in candidates:
        if n % c == 0:
            return c
    raise ValueError(f"no tile for {n}")


def _in_proj(larr, h, w_in_b, qkw, cos_t, sin_t):
    t, d = h.shape
    seg = w_in_b.shape[2] // 6
    tm = _pick_tile(t, (768, 256))
    return pl.pallas_call(
        functools.partial(_in_proj_kernel, seg=seg),
        grid_spec=pltpu.PrefetchScalarGridSpec(
            num_scalar_prefetch=1, grid=(6, t // tm),
            in_specs=[pl.BlockSpec((tm, d), lambda j, i, l: (i, 0)),
                      pl.BlockSpec((1, d, seg), lambda j, i, l: (l[0], 0, j)),
                      pl.BlockSpec((1, 8, HEAD_DIM), lambda j, i, l: (l[0], 0, 0)),
                      pl.BlockSpec((tm, HEAD_DIM), lambda j, i, l: (i, 0)),
                      pl.BlockSpec((tm, HEAD_DIM), lambda j, i, l: (i, 0))],
            out_specs=pl.BlockSpec((tm, seg), lambda j, i, l: (i, j)),
            scratch_shapes=[pltpu.VMEM((tm, seg), F32)]),
        out_shape=jax.ShapeDtypeStruct((t, 6 * seg), BF16),
        compiler_params=_cparams(("arbitrary", "arbitrary")),
        name="in_proj",
    )(larr, h, w_in_b, qkw, cos_t, sin_t)


DIFF_TAB = 3
MAX_STATIC_SHIFT = 60.0


def _diff_attn_kernel(l_ref, lam_ref, q_ref, k_ref, v_ref, sw_ref, o_ref, m_ref, s_ref, part_ref,
                      acc_ref, *, ctx_len, tk, unroll):
    i = pl.program_id(1)
    layer = l_ref[0]
    n_lat = (k_ref.shape[0] - ctx_len) // tk
    lam = lam_ref[DIFF_TAB * layer]
    out_scale = lam_ref[DIFF_TAB * layer + 1]
    bound = lam_ref[DIFF_TAB * layer + 2]
    lane = lax.broadcasted_iota(jnp.int32, (1, HEAD_DIM), 1)
    in_map0 = (lane & 32) == 0
    q = q_ref[...]
    qs = (jnp.where(in_map0, q, jnp.zeros_like(q)), jnp.where(in_map0, jnp.zeros_like(q), q))

    def lat_chunk(c):
        off = pl.multiple_of(ctx_len + c * tk, math.gcd(ctx_len, tk))
        return k_ref[pl.ds(off, tk), :], v_ref[pl.ds(off, tk), :]

    def finish(l0, l1):
        o = acc_ref[0] / l0 - lam * (acc_ref[1] / l1)
        o = o * lax.rsqrt(jnp.mean(o * o, axis=-1, keepdims=True) + NORM_EPS) * sw_ref[0] * out_scale
        o_ref[...] = o.astype(o_ref.dtype)

    acc_ref[...] = jnp.zeros(acc_ref.shape, F32)

    @pl.when(bound >= 0.0)
    def _():
        part_ref[...] = jnp.zeros(part_ref.shape, F32)
        q_both = jnp.concatenate(qs, axis=0)
        tq = q.shape[0]

        def step(kc, vc):
            p = jnp.exp2(_nt_dot(q_both, kc) - bound)
            part = p[:, 0:HEAD_DIM]
            for j in range(1, p.shape[1] // HEAD_DIM):
                part = part + p[:, j * HEAD_DIM:(j + 1) * HEAD_DIM]
            pv = jnp.dot(p.astype(BF16), vc, preferred_element_type=F32)
            for mi in range(2):
                part_ref[mi] += part[mi * tq:(mi + 1) * tq]
                acc_ref[mi] += pv[mi * tq:(mi + 1) * tq]

        step(k_ref[0:ctx_len, :], v_ref[0:ctx_len, :])

        @pl.when(i > 0)
        def _():
            def body(c, carry):
                step(*lat_chunk(c))
                return carry
            lax.fori_loop(0, n_lat, body, 0, unroll=unroll)

        finish(jnp.sum(part_ref[0], axis=-1, keepdims=True),
               jnp.sum(part_ref[1], axis=-1, keepdims=True))

    @pl.when(bound < 0.0)
    def _():
        m_ref[...] = jnp.full(m_ref.shape, -jnp.inf, F32)
        s_ref[...] = jnp.zeros(s_ref.shape, F32)

        def step(kc, vc):
            for mi in range(2):
                s = _nt_dot(qs[mi], kc)
                m_prev = m_ref[mi]
                m_new = jnp.maximum(m_prev, jnp.max(s, axis=-1, keepdims=True))
                alpha = jnp.exp2(m_prev - m_new)
                p = jnp.exp2(s - m_new)
                s_ref[mi] = alpha * s_ref[mi] + jnp.sum(p, axis=-1, keepdims=True)
                acc_ref[mi] = alpha * acc_ref[mi] + jnp.dot(p.astype(BF16), vc,
                                                           preferred_element_type=F32)
                m_ref[mi] = m_new

        step(k_ref[0:ctx_len, :], v_ref[0:ctx_len, :])

        def body(c, carry):
            step(*lat_chunk(c))
            return carry
        lax.fori_loop(0, jnp.where(i == 0, 0, n_lat), body, 0)
        finish(s_ref[0], s_ref[1])


def _diff_attn(larr, lam_tab, proj, subln, n_heads, ctx_len):
    t = proj.shape[0]
    depth = subln.shape[0]
    tq = ROW_TILE
    tk = _pick_tile(t - ctx_len, (DIFF_TK, 256))
    n_lat = (t - ctx_len) // tk
    return pl.pallas_call(
        functools.partial(_diff_attn_kernel, ctx_len=ctx_len, tk=tk,
                          unroll=_pick_tile(n_lat, (8, 4, 2, 1))),
        grid_spec=pltpu.PrefetchScalarGridSpec(
            num_scalar_prefetch=1, grid=(n_heads, t // tq),
            in_specs=[pl.BlockSpec(memory_space=pltpu.SMEM),
                      pl.BlockSpec((tq, HEAD_DIM), lambda h, i, l: (i, h)),
                      pl.BlockSpec((t, HEAD_DIM), lambda h, i, l: (0, n_heads + h)),
                      pl.BlockSpec((t, HEAD_DIM), lambda h, i, l: (0, 2 * n_heads + h)),
                      pl.BlockSpec((1, 1, HEAD_DIM), lambda h, i, l: (l[0], 0, 0))],
            out_specs=pl.BlockSpec((tq, HEAD_DIM), lambda h, i, l: (i, h)),
            scratch_shapes=[pltpu.VMEM((2, tq, 1), F32), pltpu.VMEM((2, tq, 1), F32),
                            pltpu.VMEM((2, tq, HEAD_DIM), F32),
                            pltpu.VMEM((2, tq, HEAD_DIM), F32)]),
        out_shape=jax.ShapeDtypeStruct((t, n_heads * HEAD_DIM), BF16),
        compiler_params=_cparams(("arbitrary", "arbitrary")),
        name="diff_attn",
    )(larr, lam_tab, proj, proj, proj, subln.reshape(depth, 1, HEAD_DIM))


def _na_kernel(l_ref, q_ref, kc_ref, k0_ref, k1_ref, k2_ref, vc_ref, v0_ref, v1_ref, v2_ref,
               bias_ref, o_ref, *, n_heads):
    kw = (k0_ref, k1_ref, k2_ref)
    vw = (v0_ref, v1_ref, v2_ref)
    for h in range(n_heads):
        hs = slice(h * HEAD_DIM, (h + 1) * HEAD_DIM)
        qh = q_ref[:, hs]
        s_c = _nt_dot(qh, kc_ref[:, hs])
        s_w = [_nt_dot(qh, kw[j][:, hs]) + bias_ref[0, 0, h, :, j * ROW_TILE:(j + 1) * ROW_TILE]
               for j in range(NA_WIN_TILES)]
        m = jnp.max(s_c, axis=-1, keepdims=True)
        for s in s_w:
            m = jnp.maximum(m, jnp.max(s, axis=-1, keepdims=True))
        p_c = jnp.exp2(s_c - m)
        denom = jnp.sum(p_c, axis=-1, keepdims=True)
        o = jnp.dot(p_c.astype(BF16), vc_ref[:, hs], preferred_element_type=F32)
        for j in range(NA_WIN_TILES):
            p = jnp.exp2(s_w[j] - m)
            denom = denom + jnp.sum(p, axis=-1, keepdims=True)
            o = o + jnp.dot(p.astype(BF16), vw[j][:, hs], preferred_element_type=F32)
        o_ref[:, hs] = (o / denom).astype(o_ref.dtype)


def _na_win_base(g, n_tiles):
    return 1 + jnp.clip(g - 2, 0, n_tiles - 1 - NA_WIN_TILES)


def _na_pattern(g, n_tiles):
    return jnp.where(g == 0, 0, jnp.where(g == 1, 1, jnp.where(g == n_tiles - 1, 3, 2)))


def _na_attn(larr, proj, bias_tbl, n_heads, n_diff_heads):
    t = proj.shape[0]
    n_tiles = t // ROW_TILE
    w = n_heads * HEAD_DIM
    qcol = 3 * n_diff_heads * HEAD_DIM // w
    blk = (ROW_TILE, w)

    def win_spec(col, j):
        return pl.BlockSpec(blk, lambda g, l: (_na_win_base(g, n_tiles) + j, col))

    in_specs = [pl.BlockSpec(blk, lambda g, l: (g, qcol)),
                pl.BlockSpec(blk, lambda g, l: (0, qcol + 1))]
    in_specs += [win_spec(qcol + 1, j) for j in range(NA_WIN_TILES)]
    in_specs += [pl.BlockSpec(blk, lambda g, l: (0, qcol + 2))]
    in_specs += [win_spec(qcol + 2, j) for j in range(NA_WIN_TILES)]
    in_specs += [pl.BlockSpec((1, 1, n_heads, ROW_TILE, NA_WIN_TILES * ROW_TILE),
                              lambda g, l: (l[0], _na_pattern(g, n_tiles), 0, 0, 0))]
    return pl.pallas_call(
        functools.partial(_na_kernel, n_heads=n_heads),
        grid_spec=pltpu.PrefetchScalarGridSpec(
            num_scalar_prefetch=1, grid=(n_tiles,),
            in_specs=in_specs,
            out_specs=pl.BlockSpec(blk, lambda g, l: (g, 0))),
        out_shape=jax.ShapeDtypeStruct((t, w), BF16),
        compiler_params=_cparams(("arbitrary",)),
        name="na_attn",
    )(larr, *([proj] * 9), bias_tbl)


def _na_bias_table(rpb, n_tiles):
    depth, n_h = rpb.shape[:2]
    rows = (n_tiles - 1) * NA_ROWS
    kr = min(WIN_R, rows)
    n_dr = 2 * WIN_R - 1
    n_kr = NA_WIN_TILES * NA_ROWS
    qc = np.arange(GRID_W)[:, None]
    kc = np.arange(GRID_W)[None, :]
    cs = np.clip(qc - WIN_C // 2, 0, GRID_W - WIN_C)
    col_valid = (kc >= cs) & (kc < cs + WIN_C)
    dc = np.clip(kc - qc + (WIN_C - 1), 0, 2 * WIN_C - 2)
    slabs = jnp.take(rpb, jnp.asarray(dc.reshape(-1)), axis=3)
    slabs = slabs.reshape(depth, n_h, n_dr, GRID_W, GRID_W)
    slabs = jnp.where(col_valid[None, None, None], slabs, NEG_INF) * LOG2E
    masked = jnp.full((depth, n_h, 1, GRID_W, GRID_W), NEG_INF * LOG2E, F32)
    slabs = jnp.concatenate([slabs, masked], axis=2)
    slab_idx = np.full((4, NA_ROWS, n_kr), n_dr, np.int32)
    for p, gl in enumerate((0, 1, n_tiles - 2)):
        qr = NA_ROWS * gl + np.arange(NA_ROWS)[:, None]
        win_start = NA_ROWS * int(np.clip(gl - 1, 0, n_tiles - 1 - NA_WIN_TILES))
        krow = win_start + np.arange(n_kr)[None, :]
        rs = np.clip(qr - kr // 2, 0, rows - kr)
        row_valid = (krow >= rs) & (krow < rs + kr)
        slab_idx[p + 1] = np.where(row_valid, krow - qr + (WIN_R - 1), n_dr)
    b = jnp.take(slabs, jnp.asarray(slab_idx.reshape(-1)), axis=2)
    b = b.reshape(depth, n_h, 4, NA_ROWS, n_kr, GRID_W, GRID_W)
    b = jnp.transpose(b, (0, 2, 1, 3, 5, 4, 6))
    return b.reshape(depth, 4, n_h, ROW_TILE, NA_WIN_TILES * ROW_TILE)


def _out_router_kernel(l_ref, d_ref, n_ref, wa_ref, wb_ref, x_ref, g1_ref, nw_ref, sh_ref, sc_ref,
                       wr_ref, xo_ref, h_ref, ids_ref, gates_ref):
    is_ctx = pl.program_id(0) == 0
    acc = jnp.dot(d_ref[...], wa_ref[0], preferred_element_type=F32)
    acc = acc + jnp.dot(n_ref[...], wb_ref[0], preferred_element_type=F32)
    x = x_ref[...] + _mod_row(g1_ref, is_ctx) * acc
    xo_ref[...] = x
    y = x * lax.rsqrt(jnp.mean(x * x, axis=-1, keepdims=True) + NORM_EPS) * nw_ref[0]
    y = y * (1.0 + _mod_row(sc_ref, is_ctx)) + _mod_row(sh_ref, is_ctx)
    h_ref[...] = y

    y_hi = y.astype(BF16)
    y_lo = (y - y_hi.astype(F32)).astype(BF16)
    lg = jnp.dot(y_hi, wr_ref[0, 0], preferred_element_type=F32)
    lg = lg + (jnp.dot(y_lo, wr_ref[0, 0], preferred_element_type=F32)
               + jnp.dot(y_hi, wr_ref[0, 1], preferred_element_type=F32))

    lane = lax.broadcasted_iota(jnp.int32, lg.shape, 1).astype(F32)
    big = jnp.float32(1e9)
    is_g = (lane >= N_EXPERTS) & (lane < N_EXPERTS + N_GROUPS)
    gl = jnp.where(is_g, lg, -jnp.inf)
    gmax = jnp.max(gl, axis=-1, keepdims=True)
    gsel = jnp.min(jnp.where(gl == gmax, lane, big), axis=-1, keepdims=True) - N_EXPERTS
    g_gate = 1.0 / jnp.sum(jnp.where(is_g, jnp.exp(gl - gmax), 0.0), axis=-1, keepdims=True)
    lo = gsel * EXPERTS_PER_GROUP
    el = jnp.where((lane >= lo) & (lane < lo + EXPERTS_PER_GROUP), lg, -jnp.inf)
    e1 = jnp.max(el, axis=-1, keepdims=True)
    i1 = jnp.min(jnp.where(el == e1, lane, big), axis=-1, keepdims=True)
    el2 = jnp.where(lane == i1, -jnp.inf, el)
    e2 = jnp.max(el2, axis=-1, keepdims=True)
    i2 = jnp.min(jnp.where(el2 == e2, lane, big), axis=-1, keepdims=True)
    r = jnp.exp(e2 - e1)
    w1 = 1.0 / (1.0 + r)
    w2 = r / (1.0 + r)
    ids_ref[...] = jnp.where(lane == 0, i1, jnp.where(lane == 1, i2, 0.0)).astype(jnp.int32)
    gates_ref[...] = jnp.where(lane == 0, g_gate * w1, jnp.where(lane == 1, g_gate * w2, 0.0))


def _out_router(larr, d_out, n_out, w_out_b, xa, mods, norm2_w, wr):
    t, d = xa.shape
    half = d_out.shape[1]
    depth = norm2_w.shape[0]
    tile = lambda width: pl.BlockSpec((ROW_TILE, width), lambda i, l: (i, 0))
    return pl.pallas_call(
        _out_router_kernel,
        grid_spec=pltpu.PrefetchScalarGridSpec(
            num_scalar_prefetch=1, grid=(t // ROW_TILE,),
            in_specs=[tile(half), tile(half),
                      pl.BlockSpec((1, half, d), lambda i, l: (l[0], 0, 0)),
                      pl.BlockSpec((1, half, d), lambda i, l: (l[0], 1, 0)),
                      tile(d), _mod_spec(d, 2),
                      pl.BlockSpec((1, 1, d), lambda i, l: (l[0], 0, 0)),
                      _mod_spec(d, 3), _mod_spec(d, 4),
                      pl.BlockSpec((1, 2, d, HEAD_DIM), lambda i, l: (l[0], 0, 0, 0))],
            out_specs=[tile(d), tile(d), tile(HEAD_DIM), tile(HEAD_DIM)]),
        out_shape=[jax.ShapeDtypeStruct((t, d), F32), jax.ShapeDtypeStruct((t, d), F32),
                   jax.ShapeDtypeStruct((t, HEAD_DIM), jnp.int32),
                   jax.ShapeDtypeStruct((t, HEAD_DIM), F32)],
        compiler_params=_cparams(("arbitrary",)),
        name="out_router",
    )(larr, d_out, n_out, w_out_b, w_out_b, xa, mods, norm2_w.reshape(depth, 1, d), mods, mods, wr)


def _expert_kernel(l_ref, be_ref, st_ref, nu_ref, h_hbm, w1_ref, w3_ref, w2_ref, y_ref,
                   xbuf0, xbuf1, wb1, wb3, wb2, sem):
    b = pl.program_id(0)
    n_used = nu_ref[0]
    xbufs = (xbuf0, xbuf1)

    def gather(blk, slot):
        for r in range(MOE_BLOCK):
            tok = st_ref[blk * MOE_BLOCK + r]
            pltpu.make_async_copy(h_hbm.at[pl.ds(tok, 1), :], xbufs[slot].at[pl.ds(r, 1), :],
                                  sem.at[slot]).start()

    def wait_rows(slot):
        pltpu.make_async_copy(h_hbm.at[pl.ds(0, MOE_BLOCK), :], xbufs[slot], sem.at[slot]).wait()

    @pl.when(b == 0)
    def _():
        gather(0, 0)

    @pl.when((b < n_used) & ((b == 0) | (be_ref[b] != be_ref[jnp.maximum(b - 1, 0)])))
    def _():
        wb1[...] = w1_ref[0, 0].astype(BF16)
        wb3[...] = w3_ref[0, 0].astype(BF16)
        wb2[...] = w2_ref[0, 0].astype(BF16)

    def run(slot):
        wait_rows(slot)
        gather(jnp.minimum(b + 1, pl.num_programs(0) - 1), 1 - slot)
        xb = xbufs[slot][...].astype(BF16)
        a = jnp.dot(xb, wb1[...], preferred_element_type=F32)
        g = jnp.dot(xb, wb3[...], preferred_element_type=F32)
        mid = (a / (1.0 + jnp.exp(-a))) * g
        y_ref[...] = jnp.dot(mid.astype(BF16), wb2[...], preferred_element_type=F32)

        @pl.when(b + 1 >= n_used)
        def _():
            wait_rows(1 - slot)

    for slot in range(2):
        pl.when((b < n_used) & (b % 2 == slot))(functools.partial(run, slot))

    @pl.when(b >= n_used)
    def _():
        y_ref[...] = jnp.zeros(y_ref.shape, y_ref.dtype)


def _experts(larr, block_expert, slot_tok, n_used, h2, w1, w3, w2):
    t, d = h2.shape
    ff = w1.shape[-1]
    n_blocks = block_expert.shape[0]
    return pl.pallas_call(
        _expert_kernel,
        grid_spec=pltpu.PrefetchScalarGridSpec(
            num_scalar_prefetch=4, grid=(n_blocks,),
            in_specs=[pl.BlockSpec(memory_space=pl.ANY),
                      pl.BlockSpec((1, 1, d, ff), lambda b, l, be, st, nu: (l[0], be[b], 0, 0)),
                      pl.BlockSpec((1, 1, d, ff), lambda b, l, be, st, nu: (l[0], be[b], 0, 0)),
                      pl.BlockSpec((1, 1, ff, d), lambda b, l, be, st, nu: (l[0], be[b], 0, 0))],
            out_specs=pl.BlockSpec((MOE_BLOCK, d), lambda b, l, be, st, nu: (b, 0)),
            scratch_shapes=[pltpu.VMEM((MOE_BLOCK, d), F32), pltpu.VMEM((MOE_BLOCK, d), F32),
                            pltpu.VMEM((d, ff), BF16), pltpu.VMEM((d, ff), BF16),
                            pltpu.VMEM((ff, d), BF16),
                            pltpu.SemaphoreType.DMA((2,))]),
        out_shape=jax.ShapeDtypeStruct((n_blocks * MOE_BLOCK, d), F32),
        compiler_params=_cparams(("arbitrary",)),
        name="experts",
    )(larr, block_expert, slot_tok, n_used, h2, w1, w3, w2)


def _combine_kernel(l_ref, dest_ref, y_hbm, x_ref, gates_ref, g2_ref, o_ref, ybuf, sem):
    i = pl.program_id(0)
    is_ctx = i == 0

    def issue(r, carry):
        for k in range(2):
            slot = dest_ref[(i * ROW_TILE + r) * 2 + k]
            pltpu.make_async_copy(y_hbm.at[pl.ds(slot, 1), :],
                                  ybuf.at[pl.ds(k * ROW_TILE + r, 1), :], sem).start()
        return carry
    lax.fori_loop(0, ROW_TILE, issue, 0)
    pltpu.make_async_copy(y_hbm.at[pl.ds(0, 2 * ROW_TILE), :], ybuf, sem).wait()
    gates = gates_ref[...]
    y = gates[:, 0:1] * ybuf[0:ROW_TILE, :] + gates[:, 1:2] * ybuf[ROW_TILE:2 * ROW_TILE, :]
    o_ref[...] = x_ref[...] + _mod_row(g2_ref, is_ctx) * y


def _combine(larr, dest, y_slots, x_new, gates, mods):
    t, d = x_new.shape
    return pl.pallas_call(
        _combine_kernel,
        grid_spec=pltpu.PrefetchScalarGridSpec(
            num_scalar_prefetch=2, grid=(t // ROW_TILE,),
            in_specs=[pl.BlockSpec(memory_space=pl.ANY),
                      pl.BlockSpec((ROW_TILE, d), lambda i, l, dd: (i, 0)),
                      pl.BlockSpec((ROW_TILE, HEAD_DIM), lambda i, l, dd: (i, 0)),
                      pl.BlockSpec((1, 8, d), lambda i, l, dd: (l[0], 0, 5))],
            out_specs=pl.BlockSpec((ROW_TILE, d), lambda i, l, dd: (i, 0)),
            scratch_shapes=[pltpu.VMEM((2 * ROW_TILE, d), F32),
                            pltpu.SemaphoreType.DMA(())]),
        out_shape=jax.ShapeDtypeStruct((t, d), F32),
        compiler_params=_cparams(("arbitrary",)),
        name="combine",
    )(larr, dest, y_slots, x_new, gates, mods)


def _dispatch(ids):
    t = ids.shape[0]
    n_assign = 2 * t
    e_flat = ids.reshape(n_assign)
    onehot = (e_flat[:, None] == jnp.arange(N_EXPERTS, dtype=jnp.int32)[None, :]).astype(jnp.int32)
    csum = jnp.cumsum(onehot, axis=0)
    rank = jnp.sum(onehot * (csum - 1), axis=1)
    counts = csum[-1]
    padded = (counts + MOE_BLOCK - 1) // MOE_BLOCK * MOE_BLOCK
    padded_end = jnp.cumsum(padded)
    padded_start = padded_end - padded
    dest = (padded_start[e_flat] + rank).astype(jnp.int32)
    n_blocks = -(-n_assign // MOE_BLOCK) + N_EXPERTS
    tok = jnp.arange(n_assign, dtype=jnp.int32) // 2
    slot_tok = jnp.zeros((n_blocks * MOE_BLOCK,), jnp.int32).at[dest].set(tok)
    block_start = jnp.arange(n_blocks, dtype=jnp.int32) * MOE_BLOCK
    block_expert = jnp.clip(jnp.searchsorted(padded_end, block_start, side='right'),
                            0, N_EXPERTS - 1).astype(jnp.int32)
    n_used = (padded_end[-1:] // MOE_BLOCK).astype(jnp.int32)
    return dest, slot_tok, block_expert, n_used


def _diff_lane_fields():
    lane = np.arange(HEAD_DIM)
    part, m, half, f = lane // 64, (lane % 64) // 32, (lane % 32) // 16, lane % 16
    return part, m, half, f


def _rope_tables(ctx_len, seq):
    part, m, half, f = _diff_lane_fields()
    nf = HEAD_DIM // 8
    inv = ROPE_BASE ** (-jnp.arange(nf, dtype=F32) / nf)
    pos = jnp.arange(seq)
    prow = (pos // GRID_W).astype(F32)
    pcol = (pos % GRID_W).astype(F32)
    p = jnp.where(jnp.asarray(half)[None, :] == 0, prow[:, None], pcol[:, None])
    ang = p * inv[jnp.asarray(f)][None, :]
    sign = jnp.where(jnp.asarray(part) == 0, -1.0, 1.0).astype(F32)[None, :]
    cos = jnp.concatenate([jnp.ones((ctx_len, HEAD_DIM), F32), jnp.cos(ang)], axis=0)
    sin = jnp.concatenate([jnp.zeros((ctx_len, HEAD_DIM), F32), jnp.sin(ang) * sign], axis=0)
    return cos, sin


def _permute_diff_cols(w, n_heads):
    lead = w.shape[:-1]
    w = w.reshape(lead + (n_heads, 2, 2, 2, 16))
    nd = len(lead)
    w = jnp.transpose(w, tuple(range(nd)) + (nd, nd + 3, nd + 1, nd + 2, nd + 4))
    return w.reshape(lead + (n_heads * HEAD_DIM,))


def _qk_weight_table(diff_q_norm, diff_k_norm, na_q_norm, na_k_norm):
    part, m, half, f = _diff_lane_fields()
    src = half * 32 + part * 16 + f
    dqk = HEAD_DIM // 2
    dq = diff_q_norm[:, src] * (dqk ** -0.5 * LOG2E)
    dk = diff_k_norm[:, src]
    nq = na_q_norm * (HEAD_DIM ** -0.5 * LOG2E)
    rows = jnp.stack([dq, dk, nq, na_k_norm], axis=1)
    return jnp.concatenate([rows, jnp.zeros_like(rows)], axis=1)


def kernel(x, c, ctx, c_ctx, ada_w, ada_b, norm1_w, norm2_w, w_in, w_out, diff_q_norm, diff_k_norm,
           diff_lq1, diff_lk1, diff_lq2, diff_lk2, diff_subln, na_q_norm, na_k_norm, na_rpb,
           moe_w_group, moe_w_expert, moe_w1, moe_w3, moe_w2):
    _, seq, d = x.shape
    ctx_len = ctx.shape[1]
    depth = ada_w.shape[0]
    assert ctx_len == ROW_TILE and seq % ROW_TILE == 0 and x.shape[0] == 1
    n_heads = d // HEAD_DIM
    n_diff = n_heads // 2
    n_na = n_heads - n_diff
    seg = n_diff * HEAD_DIM
    t = ctx_len + seq
    n_tiles = t // ROW_TILE

    xa = jnp.concatenate([ctx[0], x[0]], axis=0)
    c2 = jnp.zeros((8, d), F32).at[0].set(c[0]).at[1].set(c_ctx)
    mods = _adaln(c2, ada_w, ada_b)

    w_in_b = jnp.concatenate([_permute_diff_cols(w_in[..., :seg], n_diff),
                              _permute_diff_cols(w_in[..., seg:2 * seg], n_diff),
                              w_in[..., 2 * seg:]], axis=-1).astype(BF16)
    w_out_b = w_out.astype(BF16)
    qkw = _qk_weight_table(diff_q_norm, diff_k_norm, na_q_norm, na_k_norm)
    cos_t, sin_t = _rope_tables(ctx_len, seq)
    bias_tbl = _na_bias_table(na_rpb, n_tiles)

    lam_init = jnp.asarray([0.8 - 0.6 * math.exp(-0.3 * l) for l in range(depth)], F32)
    lam = (jnp.exp(jnp.sum(diff_lq1 * diff_lk1, axis=-1)) - jnp.exp(jnp.sum(diff_lq2 * diff_lk2, axis=-1))
           + lam_init)
    bound = (HEAD_DIM // 2) * jnp.max(jnp.abs(qkw[:, 0]), axis=-1) * jnp.max(jnp.abs(qkw[:, 1]), axis=-1)
    bound = bound * 1.02 + 0.01
    bound = jnp.where(bound <= MAX_STATIC_SHIFT, bound, -1.0)
    lam_tab = jnp.stack([lam, 1.0 - lam_init, bound], axis=1).reshape(DIFF_TAB * depth)

    wr = jnp.concatenate([moe_w_expert, moe_w_group,
                          jnp.zeros((depth, d, HEAD_DIM - N_EXPERTS - N_GROUPS), F32)], axis=-1)
    wr_hi = wr.astype(BF16)
    wr_lo = (wr - wr_hi.astype(F32)).astype(BF16)
    wr_split = jnp.stack([wr_hi, wr_lo], axis=1)

    for l in range(depth):
        larr = jnp.full((1,), l, jnp.int32)
        h = _norm_mod(larr, xa, norm1_w, mods)
        proj = _in_proj(larr, h, w_in_b, qkw, cos_t, sin_t)
        d_out = _diff_attn(larr, lam_tab, proj, diff_subln, n_diff, ctx_len)
        n_out = _na_attn(larr, proj, bias_tbl, n_na, n_diff)
        x_new, h2, ids, gates = _out_router(larr, d_out, n_out, w_out_b, xa, mods, norm2_w, wr_split)
        dest, slot_tok, block_expert, n_used = _dispatch(ids[:, :2])
        y_slots = _experts(larr, block_expert, slot_tok, n_used, h2, moe_w1, moe_w3, moe_w2)
        xa = _combine(larr, dest, y_slots, x_new, gates, mods)
    return xa[ctx_len:][None]
```

```python
import functools
import math

import numpy as np
import jax
import jax.numpy as jnp
from jax import lax
from jax.experimental import pallas as pl
from jax.experimental.pallas import tpu as pltpu

F32 = jnp.float32
BF16 = jnp.bfloat16

GRID_W = 64
HEAD_DIM = 128
WIN_R = 8
WIN_C = 16
ROPE_BASE = 10000.0
N_GROUPS = 4
EXPERTS_PER_GROUP = 8
N_EXPERTS = N_GROUPS * EXPERTS_PER_GROUP
NORM_EPS = 1e-6
NEG_INF = -1e30
LOG2E = 1.4426950408889634

ROW_TILE = 256
NA_ROWS = ROW_TILE // GRID_W
NA_WIN_TILES = 3
MOE_BLOCK = 256
GATHER_CHUNK = 32
ADA_TN = 1536
DIFF_TK = 512
DIFF_TK_ALL = (768, 640, 512, 256)
VMEM_LIMIT = 56 * 1024 * 1024


def _cparams(sem, vmem=VMEM_LIMIT):
    return pltpu.CompilerParams(dimension_semantics=sem, vmem_limit_bytes=vmem)


def _nt_dot(a, b):
    return lax.dot_general(a, b, (((1,), (1,)), ((), ())), preferred_element_type=F32)


def _adaln_kernel(c_ref, w_ref, b_ref, o_ref):
    c = c_ref[...]
    a = c / (1.0 + jnp.exp(-c))
    o_ref[0] = jnp.dot(a.astype(BF16), w_ref[0].astype(BF16),
                       preferred_element_type=F32) + b_ref[0]


def _adaln(c2, ada_w, ada_b):
    depth, d, six_d = ada_w.shape
    tn = ADA_TN
    return pl.pallas_call(
        _adaln_kernel,
        grid=(depth, six_d // tn),
        in_specs=[pl.BlockSpec((8, d), lambda l, j: (0, 0)),
                  pl.BlockSpec((1, d, tn), lambda l, j: (l, 0, j)),
                  pl.BlockSpec((1, 1, tn), lambda l, j: (l, 0, j))],
        out_specs=pl.BlockSpec((1, 8, tn), lambda l, j: (l, 0, j)),
        out_shape=jax.ShapeDtypeStruct((depth, 8, six_d), F32),
        compiler_params=_cparams(("arbitrary", "arbitrary")),
        name="adaln",
    )(c2, ada_w, ada_b.reshape(depth, 1, six_d))


def _mod_spec(d, chunk):
    return pl.BlockSpec((1, 8, d), lambda i, l: (l[0], 0, chunk))


def _mod_row(ref, is_ctx):
    return jnp.where(is_ctx, ref[0, 1:2, :], ref[0, 0:1, :])


def _norm_mod_kernel(l_ref, x_ref, nw_ref, sh_ref, sc_ref, o_ref):
    is_ctx = pl.program_id(0) == 0
    x = x_ref[...]
    y = x * lax.rsqrt(jnp.mean(x * x, axis=-1, keepdims=True) + NORM_EPS) * nw_ref[0]
    y = y * (1.0 + _mod_row(sc_ref, is_ctx)) + _mod_row(sh_ref, is_ctx)
    o_ref[...] = y.astype(o_ref.dtype)


def _norm_mod(larr, xa, norm_w, mods):
    t, d = xa.shape
    depth = norm_w.shape[0]
    return pl.pallas_call(
        _norm_mod_kernel,
        grid_spec=pltpu.PrefetchScalarGridSpec(
            num_scalar_prefetch=1, grid=(t // ROW_TILE,),
            in_specs=[pl.BlockSpec((ROW_TILE, d), lambda i, l: (i, 0)),
                      pl.BlockSpec((1, 1, d), lambda i, l: (l[0], 0, 0)),
                      _mod_spec(d, 0), _mod_spec(d, 1)],
            out_specs=pl.BlockSpec((ROW_TILE, d), lambda i, l: (i, 0))),
        out_shape=jax.ShapeDtypeStruct((t, d), BF16),
        compiler_params=_cparams(("arbitrary",)),
        name="norm_mod",
    )(larr, xa, norm_w.reshape(depth, 1, d), mods, mods)


def _in_proj_kernel(l_ref, x_ref, w_ref, qkw_ref, cos_ref, sin_ref, o_ref, acc_ref, *, seg):
    j = pl.program_id(0)
    acc_ref[...] = jnp.dot(x_ref[...], w_ref[0], preferred_element_type=F32)
    n_chunks = seg // HEAD_DIM

    @pl.when((j == 2) | (j == 5))
    def _():
        o_ref[...] = acc_ref[...].astype(o_ref.dtype)

    def normed(c, group_mat, inv_n, wrow):
        xc = acc_ref[:, c * HEAD_DIM:(c + 1) * HEAD_DIM]
        ss = jnp.dot((xc * xc).astype(BF16), group_mat, preferred_element_type=F32)
        return xc * lax.rsqrt(ss * inv_n + NORM_EPS) * wrow

    @pl.when(j < 2)
    def _():
        a = lax.broadcasted_iota(jnp.int32, (HEAD_DIM, HEAD_DIM), 0)
        b = lax.broadcasted_iota(jnp.int32, (HEAD_DIM, HEAD_DIM), 1)
        group_mat = ((a & 32) == (b & 32)).astype(BF16)
        wrow = jnp.where(j == 0, qkw_ref[0, 0:1, :], qkw_ref[0, 1:2, :])
        cos = cos_ref[...]
        sin = sin_ref[...]
        for c in range(n_chunks):
            y = normed(c, group_mat, 2.0 / HEAD_DIM, wrow)
            y = y * cos + pltpu.roll(y, HEAD_DIM // 2, 1) * sin
            o_ref[:, c * HEAD_DIM:(c + 1) * HEAD_DIM] = y.astype(o_ref.dtype)

    @pl.when((j == 3) | (j == 4))
    def _():
        group_mat = jnp.ones((HEAD_DIM, HEAD_DIM), BF16)
        wrow = jnp.where(j == 3, qkw_ref[0, 2:3, :], qkw_ref[0, 3:4, :])
        for c in range(n_chunks):
            y = normed(c, group_mat, 1.0 / HEAD_DIM, wrow)
            o_ref[:, c * HEAD_DIM:(c + 1) * HEAD_DIM] = y.astype(o_ref.dtype)


def _pick_tile(n, candidates):
    for c in candidates:
        if n % c == 0:
            return c
    raise ValueError(f"no tile for {n}")


def _in_proj(larr, h, w_in_b, qkw, cos_t, sin_t):
    t, d = h.shape
    seg = w_in_b.shape[2] // 6
    tm = _pick_tile(t, (768, 256))
    return pl.pallas_call(
        functools.partial(_in_proj_kernel, seg=seg),
        grid_spec=pltpu.PrefetchScalarGridSpec(
            num_scalar_prefetch=1, grid=(6, t // tm),
            in_specs=[pl.BlockSpec((tm, d), lambda j, i, l: (i, 0)),
                      pl.BlockSpec((1, d, seg), lambda j, i, l: (l[0], 0, j)),
                      pl.BlockSpec((1, 8, HEAD_DIM), lambda j, i, l: (l[0], 0, 0)),
                      pl.BlockSpec((tm, HEAD_DIM), lambda j, i, l: (i, 0)),
                      pl.BlockSpec((tm, HEAD_DIM), lambda j, i, l: (i, 0))],
            out_specs=pl.BlockSpec((tm, seg), lambda j, i, l: (i, j)),
            scratch_shapes=[pltpu.VMEM((tm, seg), F32)]),
        out_shape=jax.ShapeDtypeStruct((t, 6 * seg), BF16),
        compiler_params=_cparams(("arbitrary", "arbitrary")),
        name="in_proj",
    )(larr, h, w_in_b, qkw, cos_t, sin_t)


DIFF_TAB = 3
MAX_STATIC_SHIFT = 60.0


def _diff_attn_kernel(l_ref, lam_ref, q_ref, k_ref, v_ref, sw_ref, o_ref, m_ref, s_ref, part_ref,
                      acc_ref, *, ctx_len, tk, tk_all):
    i = pl.program_id(1)
    layer = l_ref[0]
    n_lat = (k_ref.shape[0] - ctx_len) // tk
    lam = lam_ref[DIFF_TAB * layer]
    out_scale = lam_ref[DIFF_TAB * layer + 1]
    bound = lam_ref[DIFF_TAB * layer + 2]
    lane = lax.broadcasted_iota(jnp.int32, (1, HEAD_DIM), 1)
    in_map0 = (lane & 32) == 0
    q = q_ref[...]
    qs = (jnp.where(in_map0, q, jnp.zeros_like(q)), jnp.where(in_map0, jnp.zeros_like(q), q))

    def lat_chunk(c):
        off = pl.multiple_of(ctx_len + c * tk, math.gcd(ctx_len, tk))
        return k_ref[pl.ds(off, tk), :], v_ref[pl.ds(off, tk), :]

    def finish(l0, l1):
        o = acc_ref[0] / l0 - lam * (acc_ref[1] / l1)
        o = o * lax.rsqrt(jnp.mean(o * o, axis=-1, keepdims=True) + NORM_EPS) * sw_ref[0] * out_scale
        o_ref[...] = o.astype(o_ref.dtype)

    @pl.when(bound >= 0.0)
    def _():
        tq = q.shape[0]
        q_both = jnp.concatenate(qs, axis=0)

        def attend(n_keys, chunk):
            for lo in range(0, n_keys, chunk):
                p = jnp.exp2(_nt_dot(q_both, k_ref[lo:lo + chunk, :]) - bound)
                prt = p[:, 0:HEAD_DIM]
                for j in range(1, chunk // HEAD_DIM):
                    prt = prt + p[:, j * HEAD_DIM:(j + 1) * HEAD_DIM]
                pv = jnp.dot(p.astype(BF16), v_ref[lo:lo + chunk, :], preferred_element_type=F32)
                for mi in range(2):
                    if lo == 0:
                        part_ref[mi] = prt[mi * tq:(mi + 1) * tq]
                        acc_ref[mi] = pv[mi * tq:(mi + 1) * tq]
                    else:
                        part_ref[mi] += prt[mi * tq:(mi + 1) * tq]
                        acc_ref[mi] += pv[mi * tq:(mi + 1) * tq]

        @pl.when(i == 0)
        def _():
            attend(ctx_len, ctx_len)

        @pl.when(i > 0)
        def _():
            attend(k_ref.shape[0], tk_all)

        finish(jnp.sum(part_ref[0], axis=-1, keepdims=True),
               jnp.sum(part_ref[1], axis=-1, keepdims=True))

    @pl.when(bound < 0.0)
    def _():
        m_ref[...] = jnp.full(m_ref.shape, -jnp.inf, F32)
        s_ref[...] = jnp.zeros(s_ref.shape, F32)
        acc_ref[...] = jnp.zeros(acc_ref.shape, F32)

        def step(kc, vc):
            for mi in range(2):
                s = _nt_dot(qs[mi], kc)
                m_prev = m_ref[mi]
                m_new = jnp.maximum(m_prev, jnp.max(s, axis=-1, keepdims=True))
                alpha = jnp.exp2(m_prev - m_new)
                p = jnp.exp2(s - m_new)
                s_ref[mi] = alpha * s_ref[mi] + jnp.sum(p, axis=-1, keepdims=True)
                acc_ref[mi] = alpha * acc_ref[mi] + jnp.dot(p.astype(BF16), vc,
                                                           preferred_element_type=F32)
                m_ref[mi] = m_new

        step(k_ref[0:ctx_len, :], v_ref[0:ctx_len, :])

        def body(c, carry):
            step(*lat_chunk(c))
            return carry
        lax.fori_loop(0, jnp.where(i == 0, 0, n_lat), body, 0)
        finish(s_ref[0], s_ref[1])


def _diff_attn(larr, lam_tab, proj, subln, n_heads, ctx_len):
    t = proj.shape[0]
    depth = subln.shape[0]
    tq = ROW_TILE
    tk = _pick_tile(t - ctx_len, (DIFF_TK, 256))
    n_lat = (t - ctx_len) // tk
    return pl.pallas_call(
        functools.partial(_diff_attn_kernel, ctx_len=ctx_len, tk=tk,
                          tk_all=_pick_tile(t, DIFF_TK_ALL)),
        grid_spec=pltpu.PrefetchScalarGridSpec(
            num_scalar_prefetch=1, grid=(n_heads, t // tq),
            in_specs=[pl.BlockSpec(memory_space=pltpu.SMEM),
                      pl.BlockSpec((tq, HEAD_DIM), lambda h, i, l: (i, h)),
                      pl.BlockSpec((t, HEAD_DIM), lambda h, i, l: (0, n_heads + h)),
                      pl.BlockSpec((t, HEAD_DIM), lambda h, i, l: (0, 2 * n_heads + h)),
                      pl.BlockSpec((1, 1, HEAD_DIM), lambda h, i, l: (l[0], 0, 0))],
            out_specs=pl.BlockSpec((tq, HEAD_DIM), lambda h, i, l: (i, h)),
            scratch_shapes=[pltpu.VMEM((2, tq, 1), F32), pltpu.VMEM((2, tq, 1), F32),
                            pltpu.VMEM((2, tq, HEAD_DIM), F32),
                            pltpu.VMEM((2, tq, HEAD_DIM), F32)]),
        out_shape=jax.ShapeDtypeStruct((t, n_heads * HEAD_DIM), BF16),
        compiler_params=_cparams(("arbitrary", "arbitrary")),
        name="diff_attn",
    )(larr, lam_tab, proj, proj, proj, subln.reshape(depth, 1, HEAD_DIM))


def _na_kernel(l_ref, q_ref, kc_ref, k0_ref, k1_ref, k2_ref, vc_ref, v0_ref, v1_ref, v2_ref,
               bias_ref, o_ref, *, n_heads):
    kw = (k0_ref, k1_ref, k2_ref)
    vw = (v0_ref, v1_ref, v2_ref)
    for h in range(n_heads):
        hs = slice(h * HEAD_DIM, (h + 1) * HEAD_DIM)
        qh = q_ref[:, hs]
        s_c = _nt_dot(qh, kc_ref[:, hs])
        s_w = [_nt_dot(qh, kw[j][:, hs]) + bias_ref[0, 0, h, :, j * ROW_TILE:(j + 1) * ROW_TILE]
               for j in range(NA_WIN_TILES)]
        m = jnp.max(s_c, axis=-1, keepdims=True)
        for s in s_w:
            m = jnp.maximum(m, jnp.max(s, axis=-1, keepdims=True))
        p_c = jnp.exp2(s_c - m)
        denom = jnp.sum(p_c, axis=-1, keepdims=True)
        o = jnp.dot(p_c.astype(BF16), vc_ref[:, hs], preferred_element_type=F32)
        for j in range(NA_WIN_TILES):
            p = jnp.exp2(s_w[j] - m)
            denom = denom + jnp.sum(p, axis=-1, keepdims=True)
            o = o + jnp.dot(p.astype(BF16), vw[j][:, hs], preferred_element_type=F32)
        o_ref[:, hs] = (o / denom).astype(o_ref.dtype)


def _na_win_base(g, n_tiles):
    return 1 + jnp.clip(g - 2, 0, n_tiles - 1 - NA_WIN_TILES)


def _na_pattern(g, n_tiles):
    return jnp.where(g == 0, 0, jnp.where(g == 1, 1, jnp.where(g == n_tiles - 1, 3, 2)))


def _na_attn(larr, proj, bias_tbl, n_heads, n_diff_heads):
    t = proj.shape[0]
    n_tiles = t // ROW_TILE
    w = n_heads * HEAD_DIM
    qcol = 3 * n_diff_heads * HEAD_DIM // w
    blk = (ROW_TILE, w)

    def win_spec(col, j):
        return pl.BlockSpec(blk, lambda g, l: (_na_win_base(g, n_tiles) + j, col))

    in_specs = [pl.BlockSpec(blk, lambda g, l: (g, qcol)),
                pl.BlockSpec(blk, lambda g, l: (0, qcol + 1))]
    in_specs += [win_spec(qcol + 1, j) for j in range(NA_WIN_TILES)]
    in_specs += [pl.BlockSpec(blk, lambda g, l: (0, qcol + 2))]
    in_specs += [win_spec(qcol + 2, j) for j in range(NA_WIN_TILES)]
    in_specs += [pl.BlockSpec((1, 1, n_heads, ROW_TILE, NA_WIN_TILES * ROW_TILE),
                              lambda g, l: (l[0], _na_pattern(g, n_tiles), 0, 0, 0))]
    return pl.pallas_call(
        functools.partial(_na_kernel, n_heads=n_heads),
        grid_spec=pltpu.PrefetchScalarGridSpec(
            num_scalar_prefetch=1, grid=(n_tiles,),
            in_specs=in_specs,
            out_specs=pl.BlockSpec(blk, lambda g, l: (g, 0))),
        out_shape=jax.ShapeDtypeStruct((t, w), BF16),
        compiler_params=_cparams(("arbitrary",)),
        name="na_attn",
    )(larr, *([proj] * 9), bias_tbl)


def _na_bias_table(rpb, n_tiles):
    depth, n_h = rpb.shape[:2]
    rows = (n_tiles - 1) * NA_ROWS
    kr = min(WIN_R, rows)
    n_dr = 2 * WIN_R - 1
    n_kr = NA_WIN_TILES * NA_ROWS
    qc = np.arange(GRID_W)[:, None]
    kc = np.arange(GRID_W)[None, :]
    cs = np.clip(qc - WIN_C // 2, 0, GRID_W - WIN_C)
    col_valid = (kc >= cs) & (kc < cs + WIN_C)
    dc = np.clip(kc - qc + (WIN_C - 1), 0, 2 * WIN_C - 2)
    slabs = jnp.take(rpb, jnp.asarray(dc.reshape(-1)), axis=3)
    slabs = slabs.reshape(depth, n_h, n_dr, GRID_W, GRID_W)
    slabs = jnp.where(col_valid[None, None, None], slabs, NEG_INF) * LOG2E
    masked = jnp.full((depth, n_h, 1, GRID_W, GRID_W), NEG_INF * LOG2E, F32)
    slabs = jnp.concatenate([slabs, masked], axis=2)
    slab_idx = np.full((4, NA_ROWS, n_kr), n_dr, np.int32)
    for p, gl in enumerate((0, 1, n_tiles - 2)):
        qr = NA_ROWS * gl + np.arange(NA_ROWS)[:, None]
        win_start = NA_ROWS * int(np.clip(gl - 1, 0, n_tiles - 1 - NA_WIN_TILES))
        krow = win_start + np.arange(n_kr)[None, :]
        rs = np.clip(qr - kr // 2, 0, rows - kr)
        row_valid = (krow >= rs) & (krow < rs + kr)
        slab_idx[p + 1] = np.where(row_valid, krow - qr + (WIN_R - 1), n_dr)
    slabs = jnp.transpose(slabs, (0, 1, 3, 2, 4))
    b = jnp.take(slabs, jnp.asarray(slab_idx.reshape(-1)), axis=3)
    b = b.reshape(depth, n_h, GRID_W, 4, NA_ROWS, n_kr * GRID_W)
    b = jnp.transpose(b, (0, 3, 1, 4, 2, 5))
    return b.reshape(depth, 4, n_h, ROW_TILE, NA_WIN_TILES * ROW_TILE)


def _out_router_kernel(l_ref, d_ref, n_ref, wa_ref, wb_ref, x_ref, g1_ref, nw_ref, sh_ref, sc_ref,
                       wr_ref, xo_ref, h_ref, ids_ref, gates_ref):
    is_ctx = pl.program_id(0) == 0
    acc = jnp.dot(d_ref[...], wa_ref[0], preferred_element_type=F32)
    acc = acc + jnp.dot(n_ref[...], wb_ref[0], preferred_element_type=F32)
    x = x_ref[...] + _mod_row(g1_ref, is_ctx) * acc
    xo_ref[...] = x
    y = x * lax.rsqrt(jnp.mean(x * x, axis=-1, keepdims=True) + NORM_EPS) * nw_ref[0]
    y = y * (1.0 + _mod_row(sc_ref, is_ctx)) + _mod_row(sh_ref, is_ctx)
    h_ref[...] = y

    y_hi = y.astype(BF16)
    y_lo = (y - y_hi.astype(F32)).astype(BF16)
    lg = jnp.dot(y_hi, wr_ref[0, 0], preferred_element_type=F32)
    lg = lg + (jnp.dot(y_lo, wr_ref[0, 0], preferred_element_type=F32)
               + jnp.dot(y_hi, wr_ref[0, 1], preferred_element_type=F32))

    lane = lax.broadcasted_iota(jnp.int32, lg.shape, 1).astype(F32)
    big = jnp.float32(1e9)
    is_g = (lane >= N_EXPERTS) & (lane < N_EXPERTS + N_GROUPS)
    gl = jnp.where(is_g, lg, -jnp.inf)
    gmax = jnp.max(gl, axis=-1, keepdims=True)
    gsel = jnp.min(jnp.where(gl == gmax, lane, big), axis=-1, keepdims=True) - N_EXPERTS
    g_gate = 1.0 / jnp.sum(jnp.where(is_g, jnp.exp(gl - gmax), 0.0), axis=-1, keepdims=True)
    lo = gsel * EXPERTS_PER_GROUP
    el = jnp.where((lane >= lo) & (lane < lo + EXPERTS_PER_GROUP), lg, -jnp.inf)
    e1 = jnp.max(el, axis=-1, keepdims=True)
    i1 = jnp.min(jnp.where(el == e1, lane, big), axis=-1, keepdims=True)
    el2 = jnp.where(lane == i1, -jnp.inf, el)
    e2 = jnp.max(el2, axis=-1, keepdims=True)
    i2 = jnp.min(jnp.where(el2 == e2, lane, big), axis=-1, keepdims=True)
    r = jnp.exp(e2 - e1)
    w1 = 1.0 / (1.0 + r)
    w2 = r / (1.0 + r)
    ids_ref[...] = jnp.where(lane == 0, i1, jnp.where(lane == 1, i2, 0.0)).astype(jnp.int32)
    gates_ref[...] = jnp.where(lane == 0, g_gate * w1, jnp.where(lane == 1, g_gate * w2, 0.0))


def _out_router(larr, d_out, n_out, w_out_b, xa, mods, norm2_w, wr):
    t, d = xa.shape
    half = d_out.shape[1]
    depth = norm2_w.shape[0]
    tile = lambda width: pl.BlockSpec((ROW_TILE, width), lambda i, l: (i, 0))
    return pl.pallas_call(
        _out_router_kernel,
        grid_spec=pltpu.PrefetchScalarGridSpec(
            num_scalar_prefetch=1, grid=(t // ROW_TILE,),
            in_specs=[tile(half), tile(half),
                      pl.BlockSpec((1, half, d), lambda i, l: (l[0], 0, 0)),
                      pl.BlockSpec((1, half, d), lambda i, l: (l[0], 1, 0)),
                      tile(d), _mod_spec(d, 2),
                      pl.BlockSpec((1, 1, d), lambda i, l: (l[0], 0, 0)),
                      _mod_spec(d, 3), _mod_spec(d, 4),
                      pl.BlockSpec((1, 2, d, HEAD_DIM), lambda i, l: (l[0], 0, 0, 0))],
            out_specs=[tile(d), tile(d), tile(HEAD_DIM), tile(HEAD_DIM)]),
        out_shape=[jax.ShapeDtypeStruct((t, d), F32), jax.ShapeDtypeStruct((t, d), F32),
                   jax.ShapeDtypeStruct((t, HEAD_DIM), jnp.int32),
                   jax.ShapeDtypeStruct((t, HEAD_DIM), F32)],
        compiler_params=_cparams(("arbitrary",)),
        name="out_router",
    )(larr, d_out, n_out, w_out_b, w_out_b, xa, mods, norm2_w.reshape(depth, 1, d), mods, mods, wr)


def _expert_kernel(l_ref, be_ref, st_ref, nu_ref, h_hbm, w1_ref, w3_ref, w2_ref, y_ref,
                   xbuf0, xbuf1, wb1, wb3, wb2, sem):
    b = pl.program_id(0)
    n_blocks = pl.num_programs(0)
    xbufs = (xbuf0, xbuf1)
    n_chunks = MOE_BLOCK // GATHER_CHUNK

    def rows_of(blk):
        return jnp.where(blk < n_blocks, nu_ref[jnp.minimum(blk, n_blocks - 1)], 0)

    def gather(blk, slot):
        n_rows = rows_of(blk)
        for c in range(n_chunks):
            @pl.when(c * GATHER_CHUNK < n_rows)
            def _():
                for r in range(c * GATHER_CHUNK, (c + 1) * GATHER_CHUNK):
                    tok = st_ref[blk * MOE_BLOCK + r]
                    pltpu.make_async_copy(h_hbm.at[pl.ds(tok, 1), :],
                                          xbufs[slot].at[pl.ds(r, 1), :], sem.at[slot]).start()

    def wait_rows(blk, slot):
        n_rows = rows_of(blk)
        for c in range(n_chunks):
            @pl.when(c * GATHER_CHUNK < n_rows)
            def _():
                pltpu.make_async_copy(h_hbm.at[pl.ds(0, GATHER_CHUNK), :],
                                      xbufs[slot].at[pl.ds(c * GATHER_CHUNK, GATHER_CHUNK), :],
                                      sem.at[slot]).wait()

    @pl.when(b == 0)
    def _():
        xbuf0[...] = jnp.zeros(xbuf0.shape, F32)
        xbuf1[...] = jnp.zeros(xbuf1.shape, F32)
        gather(0, 0)

    used = rows_of(b) > 0

    @pl.when(used & ((b == 0) | (be_ref[b] != be_ref[jnp.maximum(b - 1, 0)])))
    def _():
        wb1[...] = w1_ref[0, 0].astype(BF16)
        wb3[...] = w3_ref[0, 0].astype(BF16)
        wb2[...] = w2_ref[0, 0].astype(BF16)

    def run(slot):
        wait_rows(b, slot)
        gather(b + 1, 1 - slot)
        xb = xbufs[slot][...].astype(BF16)
        a = jnp.dot(xb, wb1[...], preferred_element_type=F32)
        g = jnp.dot(xb, wb3[...], preferred_element_type=F32)
        mid = (a / (1.0 + jnp.exp(-a))) * g
        y_ref[...] = jnp.dot(mid.astype(BF16), wb2[...], preferred_element_type=F32)

    for slot in range(2):
        pl.when(used & (b % 2 == slot))(functools.partial(run, slot))

    @pl.when(jnp.logical_not(used))
    def _():
        y_ref[...] = jnp.zeros(y_ref.shape, y_ref.dtype)


def _experts(larr, block_expert, slot_tok, block_rows, h2, w1, w3, w2):
    t, d = h2.shape
    ff = w1.shape[-1]
    n_blocks = block_expert.shape[0]
    return pl.pallas_call(
        _expert_kernel,
        grid_spec=pltpu.PrefetchScalarGridSpec(
            num_scalar_prefetch=4, grid=(n_blocks,),
            in_specs=[pl.BlockSpec(memory_space=pl.ANY),
                      pl.BlockSpec((1, 1, d, ff), lambda b, l, be, st, nu: (l[0], be[b], 0, 0)),
                      pl.BlockSpec((1, 1, d, ff), lambda b, l, be, st, nu: (l[0], be[b], 0, 0)),
                      pl.BlockSpec((1, 1, ff, d), lambda b, l, be, st, nu: (l[0], be[b], 0, 0))],
            out_specs=pl.BlockSpec((MOE_BLOCK, d), lambda b, l, be, st, nu: (b, 0)),
            scratch_shapes=[pltpu.VMEM((MOE_BLOCK, d), F32), pltpu.VMEM((MOE_BLOCK, d), F32),
                            pltpu.VMEM((d, ff), BF16), pltpu.VMEM((d, ff), BF16),
                            pltpu.VMEM((ff, d), BF16),
                            pltpu.SemaphoreType.DMA((2,))]),
        out_shape=jax.ShapeDtypeStruct((n_blocks * MOE_BLOCK, d), F32),
        compiler_params=_cparams(("arbitrary",)),
        name="experts",
    )(larr, block_expert, slot_tok, block_rows, h2, w1, w3, w2)


def _combine_kernel(l_ref, dest_ref, y_hbm, x_ref, gates_ref, g2_ref, nw_ref, sh_ref, sc_ref,
                    o_ref, h_ref, ybuf, sem):
    i = pl.program_id(0)
    is_ctx = i == 0

    def issue(r, carry):
        for k in range(2):
            slot = dest_ref[(i * ROW_TILE + r) * 2 + k]
            pltpu.make_async_copy(y_hbm.at[pl.ds(slot, 1), :],
                                  ybuf.at[pl.ds(k * ROW_TILE + r, 1), :], sem).start()
        return carry
    lax.fori_loop(0, ROW_TILE, issue, 0)
    pltpu.make_async_copy(y_hbm.at[pl.ds(0, 2 * ROW_TILE), :], ybuf, sem).wait()
    gates = gates_ref[...]
    y = gates[:, 0:1] * ybuf[0:ROW_TILE, :] + gates[:, 1:2] * ybuf[ROW_TILE:2 * ROW_TILE, :]
    x = x_ref[...] + _mod_row(g2_ref, is_ctx) * y
    o_ref[...] = x
    h = x * lax.rsqrt(jnp.mean(x * x, axis=-1, keepdims=True) + NORM_EPS) * nw_ref[0]
    h = h * (1.0 + _mod_row(sc_ref, is_ctx)) + _mod_row(sh_ref, is_ctx)
    h_ref[...] = h.astype(h_ref.dtype)


def _combine(larr, dest, y_slots, x_new, gates, mods, norm1_w):
    t, d = x_new.shape
    depth = norm1_w.shape[0]
    nxt = lambda l: jnp.minimum(l[0] + 1, depth - 1)
    tile = pl.BlockSpec((ROW_TILE, d), lambda i, l, dd: (i, 0))
    return pl.pallas_call(
        _combine_kernel,
        grid_spec=pltpu.PrefetchScalarGridSpec(
            num_scalar_prefetch=2, grid=(t // ROW_TILE,),
            in_specs=[pl.BlockSpec(memory_space=pl.ANY),
                      tile,
                      pl.BlockSpec((ROW_TILE, HEAD_DIM), lambda i, l, dd: (i, 0)),
                      pl.BlockSpec((1, 8, d), lambda i, l, dd: (l[0], 0, 5)),
                      pl.BlockSpec((1, 1, d), lambda i, l, dd: (nxt(l), 0, 0)),
                      pl.BlockSpec((1, 8, d), lambda i, l, dd: (nxt(l), 0, 0)),
                      pl.BlockSpec((1, 8, d), lambda i, l, dd: (nxt(l), 0, 1))],
            out_specs=[tile, tile],
            scratch_shapes=[pltpu.VMEM((2 * ROW_TILE, d), F32),
                            pltpu.SemaphoreType.DMA(())]),
        out_shape=[jax.ShapeDtypeStruct((t, d), F32), jax.ShapeDtypeStruct((t, d), BF16)],
        compiler_params=_cparams(("arbitrary",)),
        name="combine",
    )(larr, dest, y_slots, x_new, gates, mods, norm1_w.reshape(depth, 1, d), mods, mods)


def _dispatch(ids):
    t = ids.shape[0]
    n_assign = 2 * t
    e_flat = ids.reshape(n_assign)
    onehot = (e_flat[:, None] == jnp.arange(N_EXPERTS, dtype=jnp.int32)[None, :]).astype(jnp.int32)
    csum = jnp.cumsum(onehot, axis=0)
    rank = jnp.sum(onehot * (csum - 1), axis=1)
    counts = csum[-1]
    padded = (counts + MOE_BLOCK - 1) // MOE_BLOCK * MOE_BLOCK
    padded_end = jnp.cumsum(padded)
    padded_start = padded_end - padded
    dest = (padded_start[e_flat] + rank).astype(jnp.int32)
    n_blocks = -(-n_assign // MOE_BLOCK) + N_EXPERTS
    tok = jnp.arange(n_assign, dtype=jnp.int32) // 2
    slot_tok = jnp.zeros((n_blocks * MOE_BLOCK,), jnp.int32).at[dest].set(tok)
    block_start = jnp.arange(n_blocks, dtype=jnp.int32) * MOE_BLOCK
    block_expert = jnp.clip(jnp.searchsorted(padded_end, block_start, side='right'),
                            0, N_EXPERTS - 1).astype(jnp.int32)
    filled_end = (padded_start + counts)[block_expert]
    block_rows = jnp.clip(filled_end - block_start, 0, MOE_BLOCK).astype(jnp.int32)
    return dest, slot_tok, block_expert, block_rows


def _diff_lane_fields():
    lane = np.arange(HEAD_DIM)
    part, m, half, f = lane // 64, (lane % 64) // 32, (lane % 32) // 16, lane % 16
    return part, m, half, f


def _rope_tables(ctx_len, seq):
    part, m, half, f = _diff_lane_fields()
    nf = HEAD_DIM // 8
    inv = ROPE_BASE ** (-jnp.arange(nf, dtype=F32) / nf)
    pos = jnp.arange(seq)
    prow = (pos // GRID_W).astype(F32)
    pcol = (pos % GRID_W).astype(F32)
    p = jnp.where(jnp.asarray(half)[None, :] == 0, prow[:, None], pcol[:, None])
    ang = p * inv[jnp.asarray(f)][None, :]
    sign = jnp.where(jnp.asarray(part) == 0, -1.0, 1.0).astype(F32)[None, :]
    cos = jnp.concatenate([jnp.ones((ctx_len, HEAD_DIM), F32), jnp.cos(ang)], axis=0)
    sin = jnp.concatenate([jnp.zeros((ctx_len, HEAD_DIM), F32), jnp.sin(ang) * sign], axis=0)
    return cos, sin


def _permute_diff_cols(w, n_heads):
    lead = w.shape[:-1]
    w = w.reshape(lead + (n_heads, 2, 2, 2, 16))
    nd = len(lead)
    w = jnp.transpose(w, tuple(range(nd)) + (nd, nd + 3, nd + 1, nd + 2, nd + 4))
    return w.reshape(lead + (n_heads * HEAD_DIM,))


def _qk_weight_table(diff_q_norm, diff_k_norm, na_q_norm, na_k_norm):
    part, m, half, f = _diff_lane_fields()
    src = half * 32 + part * 16 + f
    dqk = HEAD_DIM // 2
    dq = diff_q_norm[:, src] * (dqk ** -0.5 * LOG2E)
    dk = diff_k_norm[:, src]
    nq = na_q_norm * (HEAD_DIM ** -0.5 * LOG2E)
    rows = jnp.stack([dq, dk, nq, na_k_norm], axis=1)
    return jnp.concatenate([rows, jnp.zeros_like(rows)], axis=1)


def kernel(x, c, ctx, c_ctx, ada_w, ada_b, norm1_w, norm2_w, w_in, w_out, diff_q_norm, diff_k_norm,
           diff_lq1, diff_lk1, diff_lq2, diff_lk2, diff_subln, na_q_norm, na_k_norm, na_rpb,
           moe_w_group, moe_w_expert, moe_w1, moe_w3, moe_w2):
    _, seq, d = x.shape
    ctx_len = ctx.shape[1]
    depth = ada_w.shape[0]
    assert ctx_len == ROW_TILE and seq % ROW_TILE == 0 and x.shape[0] == 1
    n_heads = d // HEAD_DIM
    n_diff = n_heads // 2
    n_na = n_heads - n_diff
    seg = n_diff * HEAD_DIM
    t = ctx_len + seq
    n_tiles = t // ROW_TILE

    xa = jnp.concatenate([ctx[0], x[0]], axis=0)
    c2 = jnp.zeros((8, d), F32).at[0].set(c[0]).at[1].set(c_ctx)
    mods = _adaln(c2, ada_w, ada_b)

    w_in_b = jnp.concatenate([_permute_diff_cols(w_in[..., :seg], n_diff),
                              _permute_diff_cols(w_in[..., seg:2 * seg], n_diff),
                              w_in[..., 2 * seg:]], axis=-1).astype(BF16)
    w_out_b = w_out.astype(BF16)
    qkw = _qk_weight_table(diff_q_norm, diff_k_norm, na_q_norm, na_k_norm)
    cos_t, sin_t = _rope_tables(ctx_len, seq)
    bias_tbl = _na_bias_table(na_rpb, n_tiles)

    lam_init = jnp.asarray([0.8 - 0.6 * math.exp(-0.3 * l) for l in range(depth)], F32)
    lam = (jnp.exp(jnp.sum(diff_lq1 * diff_lk1, axis=-1)) - jnp.exp(jnp.sum(diff_lq2 * diff_lk2, axis=-1))
           + lam_init)
    bound = (HEAD_DIM // 2) * jnp.max(jnp.abs(qkw[:, 0]), axis=-1) * jnp.max(jnp.abs(qkw[:, 1]), axis=-1)
    bound = bound * 1.02 + 0.01
    bound = jnp.where(bound <= MAX_STATIC_SHIFT, bound, -1.0)
    lam_tab = jnp.stack([lam, 1.0 - lam_init, bound], axis=1).reshape(DIFF_TAB * depth)

    wr = jnp.concatenate([moe_w_expert, moe_w_group,
                          jnp.zeros((depth, d, HEAD_DIM - N_EXPERTS - N_GROUPS), F32)], axis=-1)
    wr_hi = wr.astype(BF16)
    wr_lo = (wr - wr_hi.astype(F32)).astype(BF16)
    wr_split = jnp.stack([wr_hi, wr_lo], axis=1)

    h = _norm_mod(jnp.zeros((1,), jnp.int32), xa, norm1_w, mods)
    for l in range(depth):
        larr = jnp.full((1,), l, jnp.int32)
        proj = _in_proj(larr, h, w_in_b, qkw, cos_t, sin_t)
        d_out = _diff_attn(larr, lam_tab, proj, diff_subln, n_diff, ctx_len)
        n_out = _na_attn(larr, proj, bias_tbl, n_na, n_diff)
        x_new, h2, ids, gates = _out_router(larr, d_out, n_out, w_out_b, xa, mods, norm2_w, wr_split)
        dest, slot_tok, block_expert, block_rows = _dispatch(ids[:, :2])
        y_slots = _experts(larr, block_expert, slot_tok, block_rows, h2, moe_w1, moe_w3, moe_w2)
        xa, h = _combine(larr, dest, y_slots, x_new, gates, mods, norm1_w)
    return xa[ctx_len:][None]
```

```python
import functools
import math

import numpy as np
import jax
import jax.numpy as jnp
from jax import lax
from jax.experimental import pallas as pl
from jax.experimental.pallas import tpu as pltpu

F32 = jnp.float32
BF16 = jnp.bfloat16

GRID_W = 64
HEAD_DIM = 128
WIN_R = 8
WIN_C = 16
ROPE_BASE = 10000.0
N_GROUPS = 4
EXPERTS_PER_GROUP = 8
N_EXPERTS = N_GROUPS * EXPERTS_PER_GROUP
NORM_EPS = 1e-6
NEG_INF = -1e30
LOG2E = 1.4426950408889634

ROW_TILE = 256
NA_ROWS = ROW_TILE // GRID_W
NA_WIN_TILES = 3
MOE_BLOCK = 256
GATHER_CHUNK = 32
ADA_TN = 1536
DIFF_TK = 512
DIFF_TK_ALL = (768, 640, 512, 256)
VMEM_LIMIT = 56 * 1024 * 1024


def _cparams(sem, vmem=VMEM_LIMIT):
    return pltpu.CompilerParams(dimension_semantics=sem, vmem_limit_bytes=vmem)


def _nt_dot(a, b):
    return lax.dot_general(a, b, (((1,), (1,)), ((), ())), preferred_element_type=F32)


def _adaln_kernel(c_ref, w_ref, b_ref, o_ref):
    c = c_ref[...]
    a = c / (1.0 + jnp.exp(-c))
    o_ref[0] = jnp.dot(a.astype(BF16), w_ref[0].astype(BF16),
                       preferred_element_type=F32) + b_ref[0]


def _adaln(c2, ada_w, ada_b):
    depth, d, six_d = ada_w.shape
    tn = ADA_TN
    return pl.pallas_call(
        _adaln_kernel,
        grid=(depth, six_d // tn),
        in_specs=[pl.BlockSpec((8, d), lambda l, j: (0, 0)),
                  pl.BlockSpec((1, d, tn), lambda l, j: (l, 0, j)),
                  pl.BlockSpec((1, 1, tn), lambda l, j: (l, 0, j))],
        out_specs=pl.BlockSpec((1, 8, tn), lambda l, j: (l, 0, j)),
        out_shape=jax.ShapeDtypeStruct((depth, 8, six_d), F32),
        compiler_params=_cparams(("arbitrary", "arbitrary")),
        name="adaln",
    )(c2, ada_w, ada_b.reshape(depth, 1, six_d))


def _mod_spec(d, chunk):
    return pl.BlockSpec((1, 8, d), lambda i, l: (l[0], 0, chunk))


def _mod_row(ref, is_ctx):
    return jnp.where(is_ctx, ref[0, 1:2, :], ref[0, 0:1, :])


def _norm_mod_kernel(l_ref, x_ref, nw_ref, sh_ref, sc_ref, o_ref):
    is_ctx = pl.program_id(0) == 0
    x = x_ref[...]
    y = x * lax.rsqrt(jnp.mean(x * x, axis=-1, keepdims=True) + NORM_EPS) * nw_ref[0]
    y = y * (1.0 + _mod_row(sc_ref, is_ctx)) + _mod_row(sh_ref, is_ctx)
    o_ref[...] = y.astype(o_ref.dtype)


def _norm_mod(larr, xa, norm_w, mods):
    t, d = xa.shape
    depth = norm_w.shape[0]
    return pl.pallas_call(
        _norm_mod_kernel,
        grid_spec=pltpu.PrefetchScalarGridSpec(
            num_scalar_prefetch=1, grid=(t // ROW_TILE,),
            in_specs=[pl.BlockSpec((ROW_TILE, d), lambda i, l: (i, 0)),
                      pl.BlockSpec((1, 1, d), lambda i, l: (l[0], 0, 0)),
                      _mod_spec(d, 0), _mod_spec(d, 1)],
            out_specs=pl.BlockSpec((ROW_TILE, d), lambda i, l: (i, 0))),
        out_shape=jax.ShapeDtypeStruct((t, d), BF16),
        compiler_params=_cparams(("arbitrary",)),
        name="norm_mod",
    )(larr, xa, norm_w.reshape(depth, 1, d), mods, mods)


def _in_proj_kernel(l_ref, x_ref, wq_ref, wk_ref, w_ref, qkw_ref, cos_ref, sin_ref, o_ref, acc_ref,
                    *, seg):
    j = pl.program_id(0)
    n_chunks = seg // HEAD_DIM

    for which, ref in ((j == 0, wq_ref), (j == 1, wk_ref), (j >= 2, w_ref)):
        @pl.when(which)
        def _(ref=ref):
            acc_ref[...] = jnp.dot(x_ref[...], ref[0], preferred_element_type=F32)

    @pl.when((j == 2) | (j == 5))
    def _():
        o_ref[...] = acc_ref[...].astype(o_ref.dtype)

    def normed(c, group_mat, inv_n, wrow):
        xc = acc_ref[:, c * HEAD_DIM:(c + 1) * HEAD_DIM]
        ss = jnp.dot((xc * xc).astype(BF16), group_mat, preferred_element_type=F32)
        return xc * lax.rsqrt(ss * inv_n + NORM_EPS) * wrow

    @pl.when(j < 2)
    def _():
        a = lax.broadcasted_iota(jnp.int32, (HEAD_DIM, HEAD_DIM), 0)
        b = lax.broadcasted_iota(jnp.int32, (HEAD_DIM, HEAD_DIM), 1)
        group_mat = ((a & 32) == (b & 32)).astype(BF16)
        wrow = jnp.where(j == 0, qkw_ref[0, 0:1, :], qkw_ref[0, 1:2, :])
        cos = cos_ref[...]
        sin = sin_ref[...]
        for c in range(n_chunks):
            y = normed(c, group_mat, 2.0 / HEAD_DIM, wrow)
            y = y * cos + pltpu.roll(y, HEAD_DIM // 2, 1) * sin
            o_ref[:, c * HEAD_DIM:(c + 1) * HEAD_DIM] = y.astype(o_ref.dtype)

    @pl.when((j == 3) | (j == 4))
    def _():
        group_mat = jnp.ones((HEAD_DIM, HEAD_DIM), BF16)
        wrow = jnp.where(j == 3, qkw_ref[0, 2:3, :], qkw_ref[0, 3:4, :])
        for c in range(n_chunks):
            y = normed(c, group_mat, 1.0 / HEAD_DIM, wrow)
            o_ref[:, c * HEAD_DIM:(c + 1) * HEAD_DIM] = y.astype(o_ref.dtype)


def _pick_tile(n, candidates):
    for c in candidates:
        if n % c == 0:
            return c
    raise ValueError(f"no tile for {n}")


def _in_proj(larr, h, w_q, w_k, w_in_b, qkw, cos_t, sin_t):
    t, d = h.shape
    seg = w_q.shape[2]
    tm = _pick_tile(t, (768, 256))
    return pl.pallas_call(
        functools.partial(_in_proj_kernel, seg=seg),
        grid_spec=pltpu.PrefetchScalarGridSpec(
            num_scalar_prefetch=1, grid=(6, t // tm),
            in_specs=[pl.BlockSpec((tm, d), lambda j, i, l: (i, 0)),
                      pl.BlockSpec((1, d, seg), lambda j, i, l: (l[0], 0, 0)),
                      pl.BlockSpec((1, d, seg), lambda j, i, l: (l[0], 0, 0)),
                      pl.BlockSpec((1, d, seg), lambda j, i, l: (l[0], 0, jnp.maximum(j, 2))),
                      pl.BlockSpec((1, 8, HEAD_DIM), lambda j, i, l: (l[0], 0, 0)),
                      pl.BlockSpec((tm, HEAD_DIM), lambda j, i, l: (i, 0)),
                      pl.BlockSpec((tm, HEAD_DIM), lambda j, i, l: (i, 0))],
            out_specs=pl.BlockSpec((tm, seg), lambda j, i, l: (i, j)),
            scratch_shapes=[pltpu.VMEM((tm, seg), F32)]),
        out_shape=jax.ShapeDtypeStruct((t, 6 * seg), BF16),
        compiler_params=_cparams(("arbitrary", "arbitrary")),
        name="in_proj",
    )(larr, h, w_q, w_k, w_in_b, qkw, cos_t, sin_t)


DIFF_TAB = 3
MAX_STATIC_SHIFT = 60.0


def _diff_attn_kernel(l_ref, lam_ref, q_ref, k_ref, v_ref, sw_ref, o_ref, m_ref, s_ref, part_ref,
                      acc_ref, *, ctx_len, tk, tk_all):
    i = pl.program_id(1)
    layer = l_ref[0]
    n_lat = (k_ref.shape[0] - ctx_len) // tk
    lam = lam_ref[DIFF_TAB * layer]
    out_scale = lam_ref[DIFF_TAB * layer + 1]
    bound = lam_ref[DIFF_TAB * layer + 2]
    lane = lax.broadcasted_iota(jnp.int32, (1, HEAD_DIM), 1)
    in_map0 = (lane & 32) == 0
    q = q_ref[...]
    qs = (jnp.where(in_map0, q, jnp.zeros_like(q)), jnp.where(in_map0, jnp.zeros_like(q), q))

    def lat_chunk(c):
        off = pl.multiple_of(ctx_len + c * tk, math.gcd(ctx_len, tk))
        return k_ref[pl.ds(off, tk), :], v_ref[pl.ds(off, tk), :]

    def finish(l0, l1):
        o = acc_ref[0] / l0 - lam * (acc_ref[1] / l1)
        o = o * lax.rsqrt(jnp.mean(o * o, axis=-1, keepdims=True) + NORM_EPS) * sw_ref[0] * out_scale
        o_ref[...] = o.astype(o_ref.dtype)

    @pl.when(bound >= 0.0)
    def _():
        tq = q.shape[0]
        q_both = jnp.concatenate(qs, axis=0)

        def attend(n_keys, chunk):
            for lo in range(0, n_keys, chunk):
                p = jnp.exp2(_nt_dot(q_both, k_ref[lo:lo + chunk, :]) - bound)
                prt = p[:, 0:HEAD_DIM]
                for j in range(1, chunk // HEAD_DIM):
                    prt = prt + p[:, j * HEAD_DIM:(j + 1) * HEAD_DIM]
                pv = jnp.dot(p.astype(BF16), v_ref[lo:lo + chunk, :], preferred_element_type=F32)
                for mi in range(2):
                    if lo == 0:
                        part_ref[mi] = prt[mi * tq:(mi + 1) * tq]
                        acc_ref[mi] = pv[mi * tq:(mi + 1) * tq]
                    else:
                        part_ref[mi] += prt[mi * tq:(mi + 1) * tq]
                        acc_ref[mi] += pv[mi * tq:(mi + 1) * tq]

        @pl.when(i == 0)
        def _():
            attend(ctx_len, ctx_len)

        @pl.when(i > 0)
        def _():
            attend(k_ref.shape[0], tk_all)

        finish(jnp.sum(part_ref[0], axis=-1, keepdims=True),
               jnp.sum(part_ref[1], axis=-1, keepdims=True))

    @pl.when(bound < 0.0)
    def _():
        m_ref[...] = jnp.full(m_ref.shape, -jnp.inf, F32)
        s_ref[...] = jnp.zeros(s_ref.shape, F32)
        acc_ref[...] = jnp.zeros(acc_ref.shape, F32)

        def step(kc, vc):
            for mi in range(2):
                s = _nt_dot(qs[mi], kc)
                m_prev = m_ref[mi]
                m_new = jnp.maximum(m_prev, jnp.max(s, axis=-1, keepdims=True))
                alpha = jnp.exp2(m_prev - m_new)
                p = jnp.exp2(s - m_new)
                s_ref[mi] = alpha * s_ref[mi] + jnp.sum(p, axis=-1, keepdims=True)
                acc_ref[mi] = alpha * acc_ref[mi] + jnp.dot(p.astype(BF16), vc,
                                                           preferred_element_type=F32)
                m_ref[mi] = m_new

        step(k_ref[0:ctx_len, :], v_ref[0:ctx_len, :])

        def body(c, carry):
            step(*lat_chunk(c))
            return carry
        lax.fori_loop(0, jnp.where(i == 0, 0, n_lat), body, 0)
        finish(s_ref[0], s_ref[1])


def _diff_attn(larr, lam_tab, proj, subln, n_heads, ctx_len):
    t = proj.shape[0]
    depth = subln.shape[0]
    tq = ROW_TILE
    tk = _pick_tile(t - ctx_len, (DIFF_TK, 256))
    n_lat = (t - ctx_len) // tk
    return pl.pallas_call(
        functools.partial(_diff_attn_kernel, ctx_len=ctx_len, tk=tk,
                          tk_all=_pick_tile(t, DIFF_TK_ALL)),
        grid_spec=pltpu.PrefetchScalarGridSpec(
            num_scalar_prefetch=1, grid=(n_heads, t // tq),
            in_specs=[pl.BlockSpec(memory_space=pltpu.SMEM),
                      pl.BlockSpec((tq, HEAD_DIM), lambda h, i, l: (i, h)),
                      pl.BlockSpec((t, HEAD_DIM), lambda h, i, l: (0, n_heads + h)),
                      pl.BlockSpec((t, HEAD_DIM), lambda h, i, l: (0, 2 * n_heads + h)),
                      pl.BlockSpec((1, 1, HEAD_DIM), lambda h, i, l: (l[0], 0, 0))],
            out_specs=pl.BlockSpec((tq, HEAD_DIM), lambda h, i, l: (i, h)),
            scratch_shapes=[pltpu.VMEM((2, tq, 1), F32), pltpu.VMEM((2, tq, 1), F32),
                            pltpu.VMEM((2, tq, HEAD_DIM), F32),
                            pltpu.VMEM((2, tq, HEAD_DIM), F32)]),
        out_shape=jax.ShapeDtypeStruct((t, n_heads * HEAD_DIM), BF16),
        compiler_params=_cparams(("arbitrary", "arbitrary")),
        name="diff_attn",
    )(larr, lam_tab, proj, proj, proj, subln.reshape(depth, 1, HEAD_DIM))


def _na_kernel(l_ref, idx_ref, q_ref, kc_ref, k0_ref, k1_ref, k2_ref, vc_ref, v0_ref, v1_ref, v2_ref,
               slab_ref, o_ref, *, n_heads):
    kw = (k0_ref, k1_ref, k2_ref)
    vw = (v0_ref, v1_ref, v2_ref)
    g = pl.program_id(0)
    n_tiles = pl.num_programs(0)
    pat = jnp.where(g == 0, 0, jnp.where(g == 1, 1, jnp.where(g == n_tiles - 1, 3, 2)))
    n_kr = NA_WIN_TILES * NA_ROWS
    low_lanes = lax.broadcasted_iota(jnp.int32, (GRID_W, 2 * GRID_W), 1) < GRID_W

    def bias_tile(h, j):
        rows = []
        for a in range(NA_ROWS):
            pairs = []
            for b in range(NA_ROWS * j, NA_ROWS * (j + 1), 2):
                i0 = idx_ref[(pat * NA_ROWS + a) * n_kr + b]
                i1 = idx_ref[(pat * NA_ROWS + a) * n_kr + b + 1]
                pairs.append(jnp.where(low_lanes, slab_ref[0, h, i0], slab_ref[0, h, i1]))
            rows.append(jnp.concatenate(pairs, axis=1))
        return jnp.concatenate(rows, axis=0)

    for h in range(n_heads):
        hs = slice(h * HEAD_DIM, (h + 1) * HEAD_DIM)
        qh = q_ref[:, hs]
        s_c = _nt_dot(qh, kc_ref[:, hs])
        s_w = [_nt_dot(qh, kw[j][:, hs]) + bias_tile(h, j) for j in range(NA_WIN_TILES)]
        m = jnp.max(s_c, axis=-1, keepdims=True)
        for s in s_w:
            m = jnp.maximum(m, jnp.max(s, axis=-1, keepdims=True))
        p_c = jnp.exp2(s_c - m)
        denom = jnp.sum(p_c, axis=-1, keepdims=True)
        o = jnp.dot(p_c.astype(BF16), vc_ref[:, hs], preferred_element_type=F32)
        for j in range(NA_WIN_TILES):
            p = jnp.exp2(s_w[j] - m)
            denom = denom + jnp.sum(p, axis=-1, keepdims=True)
            o = o + jnp.dot(p.astype(BF16), vw[j][:, hs], preferred_element_type=F32)
        o_ref[:, hs] = (o / denom).astype(o_ref.dtype)


def _na_win_base(g, n_tiles):
    return 1 + jnp.clip(g - 2, 0, n_tiles - 1 - NA_WIN_TILES)


def _na_attn(larr, proj, slabs, slab_idx, n_heads, n_diff_heads):
    t = proj.shape[0]
    n_tiles = t // ROW_TILE
    w = n_heads * HEAD_DIM
    qcol = 3 * n_diff_heads * HEAD_DIM // w
    blk = (ROW_TILE, w)

    def win_spec(col, j):
        return pl.BlockSpec(blk, lambda g, l, ix: (_na_win_base(g, n_tiles) + j, col))

    in_specs = [pl.BlockSpec(blk, lambda g, l, ix: (g, qcol)),
                pl.BlockSpec(blk, lambda g, l, ix: (0, qcol + 1))]
    in_specs += [win_spec(qcol + 1, j) for j in range(NA_WIN_TILES)]
    in_specs += [pl.BlockSpec(blk, lambda g, l, ix: (0, qcol + 2))]
    in_specs += [win_spec(qcol + 2, j) for j in range(NA_WIN_TILES)]
    in_specs += [pl.BlockSpec((1,) + slabs.shape[1:], lambda g, l, ix: (l[0], 0, 0, 0, 0))]
    return pl.pallas_call(
        functools.partial(_na_kernel, n_heads=n_heads),
        grid_spec=pltpu.PrefetchScalarGridSpec(
            num_scalar_prefetch=2, grid=(n_tiles,),
            in_specs=in_specs,
            out_specs=pl.BlockSpec(blk, lambda g, l, ix: (g, 0))),
        out_shape=jax.ShapeDtypeStruct((t, w), BF16),
        compiler_params=_cparams(("arbitrary",)),
        name="na_attn",
    )(larr, slab_idx, *([proj] * 9), slabs)


def _na_bias_slabs(rpb, n_tiles):
    depth, n_h = rpb.shape[:2]
    rows = (n_tiles - 1) * NA_ROWS
    kr = min(WIN_R, rows)
    n_dr = 2 * WIN_R - 1
    n_kr = NA_WIN_TILES * NA_ROWS
    qc = np.arange(GRID_W)[:, None]
    kc = np.arange(GRID_W)[None, :]
    cs = np.clip(qc - WIN_C // 2, 0, GRID_W - WIN_C)
    col_valid = (kc >= cs) & (kc < cs + WIN_C)
    pad = GRID_W - WIN_C
    row = jnp.concatenate([jnp.repeat(rpb[..., :1], pad, axis=-1), rpb,
                           jnp.repeat(rpb[..., -1:], pad, axis=-1)], axis=-1)
    slabs = jnp.stack([row[..., GRID_W - 1 - c:2 * GRID_W - 1 - c] for c in range(GRID_W)], axis=-2)
    slabs = jnp.where(col_valid[None, None, None], slabs, NEG_INF) * LOG2E
    masked = jnp.full((depth, n_h, 1, GRID_W, GRID_W), NEG_INF * LOG2E, F32)
    slabs = jnp.concatenate([slabs, masked], axis=2)
    slab_idx = np.full((4, NA_ROWS, n_kr), n_dr, np.int32)
    for p, gl in enumerate((0, 1, n_tiles - 2)):
        qr = NA_ROWS * gl + np.arange(NA_ROWS)[:, None]
        win_start = NA_ROWS * int(np.clip(gl - 1, 0, n_tiles - 1 - NA_WIN_TILES))
        krow = win_start + np.arange(n_kr)[None, :]
        rs = np.clip(qr - kr // 2, 0, rows - kr)
        row_valid = (krow >= rs) & (krow < rs + kr)
        slab_idx[p + 1] = np.where(row_valid, krow - qr + (WIN_R - 1), n_dr)
    slabs = jnp.concatenate([slabs, slabs], axis=-1)
    return slabs, jnp.asarray(slab_idx.reshape(-1))


def _out_router_kernel(l_ref, d_ref, n_ref, wa_ref, wb_ref, x_ref, g1_ref, nw_ref, sh_ref, sc_ref,
                       wr_ref, xo_ref, h_ref, ids_ref, gates_ref):
    is_ctx = pl.program_id(0) == 0
    acc = jnp.dot(d_ref[...], wa_ref[0], preferred_element_type=F32)
    acc = acc + jnp.dot(n_ref[...], wb_ref[0], preferred_element_type=F32)
    x = x_ref[...] + _mod_row(g1_ref, is_ctx) * acc
    xo_ref[...] = x
    y = x * lax.rsqrt(jnp.mean(x * x, axis=-1, keepdims=True) + NORM_EPS) * nw_ref[0]
    y = y * (1.0 + _mod_row(sc_ref, is_ctx)) + _mod_row(sh_ref, is_ctx)
    h_ref[...] = y

    y_hi = y.astype(BF16)
    y_lo = (y - y_hi.astype(F32)).astype(BF16)
    lg = jnp.dot(y_hi, wr_ref[0, 0], preferred_element_type=F32)
    lg = lg + (jnp.dot(y_lo, wr_ref[0, 0], preferred_element_type=F32)
               + jnp.dot(y_hi, wr_ref[0, 1], preferred_element_type=F32))

    lane = lax.broadcasted_iota(jnp.int32, lg.shape, 1).astype(F32)
    big = jnp.float32(1e9)
    is_g = (lane >= N_EXPERTS) & (lane < N_EXPERTS + N_GROUPS)
    gl = jnp.where(is_g, lg, -jnp.inf)
    gmax = jnp.max(gl, axis=-1, keepdims=True)
    gsel = jnp.min(jnp.where(gl == gmax, lane, big), axis=-1, keepdims=True) - N_EXPERTS
    g_gate = 1.0 / jnp.sum(jnp.where(is_g, jnp.exp(gl - gmax), 0.0), axis=-1, keepdims=True)
    lo = gsel * EXPERTS_PER_GROUP
    el = jnp.where((lane >= lo) & (lane < lo + EXPERTS_PER_GROUP), lg, -jnp.inf)
    e1 = jnp.max(el, axis=-1, keepdims=True)
    i1 = jnp.min(jnp.where(el == e1, lane, big), axis=-1, keepdims=True)
    el2 = jnp.where(lane == i1, -jnp.inf, el)
    e2 = jnp.max(el2, axis=-1, keepdims=True)
    i2 = jnp.min(jnp.where(el2 == e2, lane, big), axis=-1, keepdims=True)
    r = jnp.exp(e2 - e1)
    w1 = 1.0 / (1.0 + r)
    w2 = r / (1.0 + r)
    ids_ref[...] = jnp.where(lane == 0, i1, jnp.where(lane == 1, i2, 0.0)).astype(jnp.int32)
    gates_ref[...] = jnp.where(lane == 0, g_gate * w1, jnp.where(lane == 1, g_gate * w2, 0.0))


def _out_router(larr, d_out, n_out, w_out_b, xa, mods, norm2_w, wr):
    t, d = xa.shape
    half = d_out.shape[1]
    depth = norm2_w.shape[0]
    tile = lambda width: pl.BlockSpec((ROW_TILE, width), lambda i, l: (i, 0))
    return pl.pallas_call(
        _out_router_kernel,
        grid_spec=pltpu.PrefetchScalarGridSpec(
            num_scalar_prefetch=1, grid=(t // ROW_TILE,),
            in_specs=[tile(half), tile(half),
                      pl.BlockSpec((1, half, d), lambda i, l: (l[0], 0, 0)),
                      pl.BlockSpec((1, half, d), lambda i, l: (l[0], 1, 0)),
                      tile(d), _mod_spec(d, 2),
                      pl.BlockSpec((1, 1, d), lambda i, l: (l[0], 0, 0)),
                      _mod_spec(d, 3), _mod_spec(d, 4),
                      pl.BlockSpec((1, 2, d, HEAD_DIM), lambda i, l: (l[0], 0, 0, 0))],
            out_specs=[tile(d), tile(d), tile(HEAD_DIM), tile(HEAD_DIM)]),
        out_shape=[jax.ShapeDtypeStruct((t, d), F32), jax.ShapeDtypeStruct((t, d), F32),
                   jax.ShapeDtypeStruct((t, HEAD_DIM), jnp.int32),
                   jax.ShapeDtypeStruct((t, HEAD_DIM), F32)],
        compiler_params=_cparams(("arbitrary",)),
        name="out_router",
    )(larr, d_out, n_out, w_out_b, w_out_b, xa, mods, norm2_w.reshape(depth, 1, d), mods, mods, wr)


def _expert_kernel(l_ref, be_ref, st_ref, nu_ref, h_hbm, w1_ref, w3_ref, w2_ref, y_ref,
                   xbuf0, xbuf1, wb1, wb3, wb2, sem):
    b = pl.program_id(0)
    n_blocks = pl.num_programs(0)
    xbufs = (xbuf0, xbuf1)
    n_chunks = MOE_BLOCK // GATHER_CHUNK

    def rows_of(blk):
        return jnp.where(blk < n_blocks, nu_ref[jnp.minimum(blk, n_blocks - 1)], 0)

    def gather(blk, slot):
        n_rows = rows_of(blk)
        for c in range(n_chunks):
            @pl.when(c * GATHER_CHUNK < n_rows)
            def _():
                for r in range(c * GATHER_CHUNK, (c + 1) * GATHER_CHUNK):
                    tok = st_ref[blk * MOE_BLOCK + r]
                    pltpu.make_async_copy(h_hbm.at[pl.ds(tok, 1), :],
                                          xbufs[slot].at[pl.ds(r, 1), :], sem.at[slot]).start()

    def wait_rows(blk, slot):
        n_rows = rows_of(blk)
        for c in range(n_chunks):
            @pl.when(c * GATHER_CHUNK < n_rows)
            def _():
                pltpu.make_async_copy(h_hbm.at[pl.ds(0, GATHER_CHUNK), :],
                                      xbufs[slot].at[pl.ds(c * GATHER_CHUNK, GATHER_CHUNK), :],
                                      sem.at[slot]).wait()

    @pl.when(b == 0)
    def _():
        xbuf0[...] = jnp.zeros(xbuf0.shape, F32)
        xbuf1[...] = jnp.zeros(xbuf1.shape, F32)
        gather(0, 0)

    used = rows_of(b) > 0

    @pl.when(used & ((b == 0) | (be_ref[b] != be_ref[jnp.maximum(b - 1, 0)])))
    def _():
        wb1[...] = w1_ref[0, 0].astype(BF16)
        wb3[...] = w3_ref[0, 0].astype(BF16)
        wb2[...] = w2_ref[0, 0].astype(BF16)

    def run(slot):
        wait_rows(b, slot)
        gather(b + 1, 1 - slot)
        xb = xbufs[slot][...].astype(BF16)
        a = jnp.dot(xb, wb1[...], preferred_element_type=F32)
        g = jnp.dot(xb, wb3[...], preferred_element_type=F32)
        mid = (a / (1.0 + jnp.exp(-a))) * g
        y_ref[...] = jnp.dot(mid.astype(BF16), wb2[...], preferred_element_type=F32)

    for slot in range(2):
        pl.when(used & (b % 2 == slot))(functools.partial(run, slot))

    @pl.when(jnp.logical_not(used))
    def _():
        y_ref[...] = jnp.zeros(y_ref.shape, y_ref.dtype)


def _experts(larr, block_expert, slot_tok, block_rows, h2, w1, w3, w2):
    t, d = h2.shape
    ff = w1.shape[-1]
    n_blocks = block_expert.shape[0]
    return pl.pallas_call(
        _expert_kernel,
        grid_spec=pltpu.PrefetchScalarGridSpec(
            num_scalar_prefetch=4, grid=(n_blocks,),
            in_specs=[pl.BlockSpec(memory_space=pl.ANY),
                      pl.BlockSpec((1, 1, d, ff), lambda b, l, be, st, nu: (l[0], be[b], 0, 0)),
                      pl.BlockSpec((1, 1, d, ff), lambda b, l, be, st, nu: (l[0], be[b], 0, 0)),
                      pl.BlockSpec((1, 1, ff, d), lambda b, l, be, st, nu: (l[0], be[b], 0, 0))],
            out_specs=pl.BlockSpec((MOE_BLOCK, d), lambda b, l, be, st, nu: (b, 0)),
            scratch_shapes=[pltpu.VMEM((MOE_BLOCK, d), F32), pltpu.VMEM((MOE_BLOCK, d), F32),
                            pltpu.VMEM((d, ff), BF16), pltpu.VMEM((d, ff), BF16),
                            pltpu.VMEM((ff, d), BF16),
                            pltpu.SemaphoreType.DMA((2,))]),
        out_shape=jax.ShapeDtypeStruct((n_blocks * MOE_BLOCK, d), F32),
        compiler_params=_cparams(("arbitrary",)),
        name="experts",
    )(larr, block_expert, slot_tok, block_rows, h2, w1, w3, w2)


def _combine_kernel(l_ref, dest_ref, y_hbm, x_ref, gates_ref, g2_ref, nw_ref, sh_ref, sc_ref,
                    o_ref, h_ref, ybuf, sem):
    i = pl.program_id(0)
    is_ctx = i == 0

    def issue(r, carry):
        for k in range(2):
            slot = dest_ref[(i * ROW_TILE + r) * 2 + k]
            pltpu.make_async_copy(y_hbm.at[pl.ds(slot, 1), :],
                                  ybuf.at[pl.ds(k * ROW_TILE + r, 1), :], sem).start()
        return carry
    lax.fori_loop(0, ROW_TILE, issue, 0)
    pltpu.make_async_copy(y_hbm.at[pl.ds(0, 2 * ROW_TILE), :], ybuf, sem).wait()
    gates = gates_ref[...]
    y = gates[:, 0:1] * ybuf[0:ROW_TILE, :] + gates[:, 1:2] * ybuf[ROW_TILE:2 * ROW_TILE, :]
    x = x_ref[...] + _mod_row(g2_ref, is_ctx) * y
    o_ref[...] = x
    h = x * lax.rsqrt(jnp.mean(x * x, axis=-1, keepdims=True) + NORM_EPS) * nw_ref[0]
    h = h * (1.0 + _mod_row(sc_ref, is_ctx)) + _mod_row(sh_ref, is_ctx)
    h_ref[...] = h.astype(h_ref.dtype)


def _combine(larr, dest, y_slots, x_new, gates, mods, norm1_w):
    t, d = x_new.shape
    depth = norm1_w.shape[0]
    nxt = lambda l: jnp.minimum(l[0] + 1, depth - 1)
    tile = pl.BlockSpec((ROW_TILE, d), lambda i, l, dd: (i, 0))
    return pl.pallas_call(
        _combine_kernel,
        grid_spec=pltpu.PrefetchScalarGridSpec(
            num_scalar_prefetch=2, grid=(t // ROW_TILE,),
            in_specs=[pl.BlockSpec(memory_space=pl.ANY),
                      tile,
                      pl.BlockSpec((ROW_TILE, HEAD_DIM), lambda i, l, dd: (i, 0)),
                      pl.BlockSpec((1, 8, d), lambda i, l, dd: (l[0], 0, 5)),
                      pl.BlockSpec((1, 1, d), lambda i, l, dd: (nxt(l), 0, 0)),
                      pl.BlockSpec((1, 8, d), lambda i, l, dd: (nxt(l), 0, 0)),
                      pl.BlockSpec((1, 8, d), lambda i, l, dd: (nxt(l), 0, 1))],
            out_specs=[tile, tile],
            scratch_shapes=[pltpu.VMEM((2 * ROW_TILE, d), F32),
                            pltpu.SemaphoreType.DMA(())]),
        out_shape=[jax.ShapeDtypeStruct((t, d), F32), jax.ShapeDtypeStruct((t, d), BF16)],
        compiler_params=_cparams(("arbitrary",)),
        name="combine",
    )(larr, dest, y_slots, x_new, gates, mods, norm1_w.reshape(depth, 1, d), mods, mods)


def _dispatch(ids):
    t = ids.shape[0]
    n_assign = 2 * t
    e_flat = ids.reshape(n_assign)
    onehot = (e_flat[:, None] == jnp.arange(N_EXPERTS, dtype=jnp.int32)[None, :]).astype(jnp.int32)
    csum = jnp.cumsum(onehot, axis=0)
    rank = jnp.sum(onehot * (csum - 1), axis=1)
    counts = csum[-1]
    padded = (counts + MOE_BLOCK - 1) // MOE_BLOCK * MOE_BLOCK
    padded_end = jnp.cumsum(padded)
    padded_start = padded_end - padded
    dest = (padded_start[e_flat] + rank).astype(jnp.int32)
    n_blocks = -(-n_assign // MOE_BLOCK) + N_EXPERTS
    tok = jnp.arange(n_assign, dtype=jnp.int32) // 2
    slot_tok = jnp.zeros((n_blocks * MOE_BLOCK,), jnp.int32).at[dest].set(tok)
    block_start = jnp.arange(n_blocks, dtype=jnp.int32) * MOE_BLOCK
    block_expert = jnp.clip(jnp.searchsorted(padded_end, block_start, side='right'),
                            0, N_EXPERTS - 1).astype(jnp.int32)
    filled_end = (padded_start + counts)[block_expert]
    block_rows = jnp.clip(filled_end - block_start, 0, MOE_BLOCK).astype(jnp.int32)
    return dest, slot_tok, block_expert, block_rows


def _diff_lane_fields():
    lane = np.arange(HEAD_DIM)
    part, m, half, f = lane // 64, (lane % 64) // 32, (lane % 32) // 16, lane % 16
    return part, m, half, f


def _rope_tables(ctx_len, seq):
    part, m, half, f = _diff_lane_fields()
    nf = HEAD_DIM // 8
    inv = ROPE_BASE ** (-jnp.arange(nf, dtype=F32) / nf)
    pos = jnp.arange(seq)
    prow = (pos // GRID_W).astype(F32)
    pcol = (pos % GRID_W).astype(F32)
    p = jnp.where(jnp.asarray(half)[None, :] == 0, prow[:, None], pcol[:, None])
    ang = p * inv[jnp.asarray(f)][None, :]
    sign = jnp.where(jnp.asarray(part) == 0, -1.0, 1.0).astype(F32)[None, :]
    cos = jnp.concatenate([jnp.ones((ctx_len, HEAD_DIM), F32), jnp.cos(ang)], axis=0)
    sin = jnp.concatenate([jnp.zeros((ctx_len, HEAD_DIM), F32), jnp.sin(ang) * sign], axis=0)
    return cos, sin


def _permute_diff_cols(w, n_heads):
    lead = w.shape[:-1]
    w = w.reshape(lead + (n_heads, 2, 2, 2, 16))
    nd = len(lead)
    w = jnp.transpose(w, tuple(range(nd)) + (nd, nd + 3, nd + 1, nd + 2, nd + 4))
    return w.reshape(lead + (n_heads * HEAD_DIM,))


def _qk_weight_table(diff_q_norm, diff_k_norm, na_q_norm, na_k_norm):
    part, m, half, f = _diff_lane_fields()
    src = half * 32 + part * 16 + f
    dqk = HEAD_DIM // 2
    dq = diff_q_norm[:, src] * (dqk ** -0.5 * LOG2E)
    dk = diff_k_norm[:, src]
    nq = na_q_norm * (HEAD_DIM ** -0.5 * LOG2E)
    rows = jnp.stack([dq, dk, nq, na_k_norm], axis=1)
    return jnp.concatenate([rows, jnp.zeros_like(rows)], axis=1)


def kernel(x, c, ctx, c_ctx, ada_w, ada_b, norm1_w, norm2_w, w_in, w_out, diff_q_norm, diff_k_norm,
           diff_lq1, diff_lk1, diff_lq2, diff_lk2, diff_subln, na_q_norm, na_k_norm, na_rpb,
           moe_w_group, moe_w_expert, moe_w1, moe_w3, moe_w2):
    _, seq, d = x.shape
    ctx_len = ctx.shape[1]
    depth = ada_w.shape[0]
    assert ctx_len == ROW_TILE and seq % ROW_TILE == 0 and x.shape[0] == 1
    n_heads = d // HEAD_DIM
    n_diff = n_heads // 2
    n_na = n_heads - n_diff
    seg = n_diff * HEAD_DIM
    t = ctx_len + seq
    n_tiles = t // ROW_TILE

    xa = jnp.concatenate([ctx[0], x[0]], axis=0)
    c2 = jnp.zeros((8, d), F32).at[0].set(c[0]).at[1].set(c_ctx)
    mods = _adaln(c2, ada_w, ada_b)

    w_in_b = w_in.astype(BF16)
    w_q = _permute_diff_cols(w_in_b[..., :seg], n_diff)
    w_k = _permute_diff_cols(w_in_b[..., seg:2 * seg], n_diff)
    w_out_b = w_out.astype(BF16)
    qkw = _qk_weight_table(diff_q_norm, diff_k_norm, na_q_norm, na_k_norm)
    cos_t, sin_t = _rope_tables(ctx_len, seq)
    na_slabs, na_slab_idx = _na_bias_slabs(na_rpb, n_tiles)

    lam_init = jnp.asarray([0.8 - 0.6 * math.exp(-0.3 * l) for l in range(depth)], F32)
    lam = (jnp.exp(jnp.sum(diff_lq1 * diff_lk1, axis=-1)) - jnp.exp(jnp.sum(diff_lq2 * diff_lk2, axis=-1))
           + lam_init)
    bound = (HEAD_DIM // 2) * jnp.max(jnp.abs(qkw[:, 0]), axis=-1) * jnp.max(jnp.abs(qkw[:, 1]), axis=-1)
    bound = bound * 1.02 + 0.01
    bound = jnp.where(bound <= MAX_STATIC_SHIFT, bound, -1.0)
    lam_tab = jnp.stack([lam, 1.0 - lam_init, bound], axis=1).reshape(DIFF_TAB * depth)

    wr = jnp.concatenate([moe_w_expert, moe_w_group,
                          jnp.zeros((depth, d, HEAD_DIM - N_EXPERTS - N_GROUPS), F32)], axis=-1)
    wr_hi = wr.astype(BF16)
    wr_lo = (wr - wr_hi.astype(F32)).astype(BF16)
    wr_split = jnp.stack([wr_hi, wr_lo], axis=1)

    h = _norm_mod(jnp.zeros((1,), jnp.int32), xa, norm1_w, mods)
    for l in range(depth):
        larr = jnp.full((1,), l, jnp.int32)
        proj = _in_proj(larr, h, w_q, w_k, w_in_b, qkw, cos_t, sin_t)
        d_out = _diff_attn(larr, lam_tab, proj, diff_subln, n_diff, ctx_len)
        n_out = _na_attn(larr, proj, na_slabs, na_slab_idx, n_na, n_diff)
        x_new, h2, ids, gates = _out_router(larr, d_out, n_out, w_out_b, xa, mods, norm2_w, wr_split)
        dest, slot_tok, block_expert, block_rows = _dispatch(ids[:, :2])
        y_slots = _experts(larr, block_expert, slot_tok, block_rows, h2, moe_w1, moe_w3, moe_w2)
        xa, h = _combine(larr, dest, y_slots, x_new, gates, mods, norm1_w)
    return xa[ctx_len:][None]
```

```python
import functools
import math

import numpy as np
import jax
import jax.numpy as jnp
from jax import lax
from jax.experimental import pallas as pl
from jax.experimental.pallas import tpu as pltpu

F32 = jnp.float32
BF16 = jnp.bfloat16

GRID_W = 64
HEAD_DIM = 128
WIN_R = 8
WIN_C = 16
ROPE_BASE = 10000.0
N_GROUPS = 4
EXPERTS_PER_GROUP = 8
N_EXPERTS = N_GROUPS * EXPERTS_PER_GROUP
NORM_EPS = 1e-6
NEG_INF = -1e30
LOG2E = 1.4426950408889634

ROW_TILE = 256
NA_ROWS = ROW_TILE // GRID_W
NA_WIN_TILES = 3
MOE_BLOCK = 256
GATHER_CHUNK = 32
ADA_TN = 1536
DIFF_TK = 512
DIFF_TK_ALL = (768, 640, 512, 256)
VMEM_LIMIT = 56 * 1024 * 1024


def _cparams(sem, vmem=VMEM_LIMIT):
    return pltpu.CompilerParams(dimension_semantics=sem, vmem_limit_bytes=vmem)


def _nt_dot(a, b):
    return lax.dot_general(a, b, (((1,), (1,)), ((), ())), preferred_element_type=F32)


def _adaln_kernel(c_ref, w_ref, b_ref, o_ref):
    c = c_ref[...]
    a = c / (1.0 + jnp.exp(-c))
    o_ref[0] = jnp.dot(a.astype(BF16), w_ref[0].astype(BF16),
                       preferred_element_type=F32) + b_ref[0]


def _adaln(c2, ada_w, ada_b):
    depth, d, six_d = ada_w.shape
    tn = ADA_TN
    return pl.pallas_call(
        _adaln_kernel,
        grid=(depth, six_d // tn),
        in_specs=[pl.BlockSpec((8, d), lambda l, j: (0, 0)),
                  pl.BlockSpec((1, d, tn), lambda l, j: (l, 0, j)),
                  pl.BlockSpec((1, 1, tn), lambda l, j: (l, 0, j))],
        out_specs=pl.BlockSpec((1, 8, tn), lambda l, j: (l, 0, j)),
        out_shape=jax.ShapeDtypeStruct((depth, 8, six_d), F32),
        compiler_params=_cparams(("arbitrary", "arbitrary")),
        name="adaln",
    )(c2, ada_w, ada_b.reshape(depth, 1, six_d))


def _mod_spec(d, chunk):
    return pl.BlockSpec((1, 8, d), lambda i, l: (l[0], 0, chunk))


def _mod_row(ref, is_ctx):
    return jnp.where(is_ctx, ref[0, 1:2, :], ref[0, 0:1, :])


def _norm_mod_kernel(l_ref, x_ref, nw_ref, sh_ref, sc_ref, o_ref):
    is_ctx = pl.program_id(0) == 0
    x = x_ref[...]
    y = x * lax.rsqrt(jnp.mean(x * x, axis=-1, keepdims=True) + NORM_EPS) * nw_ref[0]
    y = y * (1.0 + _mod_row(sc_ref, is_ctx)) + _mod_row(sh_ref, is_ctx)
    o_ref[...] = y.astype(o_ref.dtype)


def _norm_mod(larr, xa, norm_w, mods):
    t, d = xa.shape
    depth = norm_w.shape[0]
    return pl.pallas_call(
        _norm_mod_kernel,
        grid_spec=pltpu.PrefetchScalarGridSpec(
            num_scalar_prefetch=1, grid=(t // ROW_TILE,),
            in_specs=[pl.BlockSpec((ROW_TILE, d), lambda i, l: (i, 0)),
                      pl.BlockSpec((1, 1, d), lambda i, l: (l[0], 0, 0)),
                      _mod_spec(d, 0), _mod_spec(d, 1)],
            out_specs=pl.BlockSpec((ROW_TILE, d), lambda i, l: (i, 0))),
        out_shape=jax.ShapeDtypeStruct((t, d), BF16),
        compiler_params=_cparams(("arbitrary",)),
        name="norm_mod",
    )(larr, xa, norm_w.reshape(depth, 1, d), mods, mods)


def _in_proj_kernel(l_ref, x_ref, wq_ref, wk_ref, w_ref, qkw_ref, cos_ref, sin_ref, o_ref, acc_ref,
                    *, seg):
    j = pl.program_id(0)
    n_chunks = seg // HEAD_DIM

    for which, ref in ((j == 0, wq_ref), (j == 1, wk_ref), (j >= 2, w_ref)):
        @pl.when(which)
        def _(ref=ref):
            acc_ref[...] = jnp.dot(x_ref[...], ref[0], preferred_element_type=F32)

    @pl.when((j == 2) | (j == 5))
    def _():
        o_ref[...] = acc_ref[...].astype(o_ref.dtype)

    def normed(c, group_mat, inv_n, wrow):
        xc = acc_ref[:, c * HEAD_DIM:(c + 1) * HEAD_DIM]
        ss = jnp.dot((xc * xc).astype(BF16), group_mat, preferred_element_type=F32)
        return xc * lax.rsqrt(ss * inv_n + NORM_EPS) * wrow

    @pl.when(j < 2)
    def _():
        a = lax.broadcasted_iota(jnp.int32, (HEAD_DIM, HEAD_DIM), 0)
        b = lax.broadcasted_iota(jnp.int32, (HEAD_DIM, HEAD_DIM), 1)
        group_mat = ((a & 32) == (b & 32)).astype(BF16)
        wrow = jnp.where(j == 0, qkw_ref[0, 0:1, :], qkw_ref[0, 1:2, :])
        cos = cos_ref[...]
        sin = sin_ref[...]
        for c in range(n_chunks):
            y = normed(c, group_mat, 2.0 / HEAD_DIM, wrow)
            y = y * cos + pltpu.roll(y, HEAD_DIM // 2, 1) * sin
            o_ref[:, c * HEAD_DIM:(c + 1) * HEAD_DIM] = y.astype(o_ref.dtype)

    @pl.when((j == 3) | (j == 4))
    def _():
        group_mat = jnp.ones((HEAD_DIM, HEAD_DIM), BF16)
        wrow = jnp.where(j == 3, qkw_ref[0, 2:3, :], qkw_ref[0, 3:4, :])
        for c in range(n_chunks):
            y = normed(c, group_mat, 1.0 / HEAD_DIM, wrow)
            o_ref[:, c * HEAD_DIM:(c + 1) * HEAD_DIM] = y.astype(o_ref.dtype)


def _pick_tile(n, candidates):
    for c in candidates:
        if n % c == 0:
            return c
    raise ValueError(f"no tile for {n}")


def _in_proj(larr, h, w_q, w_k, w_in_b, qkw, cos_t, sin_t):
    t, d = h.shape
    seg = w_q.shape[2]
    tm = _pick_tile(t, (768, 256))
    return pl.pallas_call(
        functools.partial(_in_proj_kernel, seg=seg),
        grid_spec=pltpu.PrefetchScalarGridSpec(
            num_scalar_prefetch=1, grid=(6, t // tm),
            in_specs=[pl.BlockSpec((tm, d), lambda j, i, l: (i, 0)),
                      pl.BlockSpec((1, d, seg), lambda j, i, l: (l[0], 0, 0)),
                      pl.BlockSpec((1, d, seg), lambda j, i, l: (l[0], 0, 0)),
                      pl.BlockSpec((1, d, seg), lambda j, i, l: (l[0], 0, jnp.maximum(j, 2))),
                      pl.BlockSpec((1, 8, HEAD_DIM), lambda j, i, l: (l[0], 0, 0)),
                      pl.BlockSpec((tm, HEAD_DIM), lambda j, i, l: (i, 0)),
                      pl.BlockSpec((tm, HEAD_DIM), lambda j, i, l: (i, 0))],
            out_specs=pl.BlockSpec((tm, seg), lambda j, i, l: (i, j)),
            scratch_shapes=[pltpu.VMEM((tm, seg), F32)]),
        out_shape=jax.ShapeDtypeStruct((t, 6 * seg), BF16),
        compiler_params=_cparams(("arbitrary", "arbitrary")),
        name="in_proj",
    )(larr, h, w_q, w_k, w_in_b, qkw, cos_t, sin_t)


DIFF_TAB = 3
MAX_STATIC_SHIFT = 60.0


def _diff_attn_kernel(l_ref, lam_ref, q_ref, k_ref, v_ref, sw_ref, o_ref, m_ref, s_ref, part_ref,
                      acc_ref, *, ctx_len, tk, tk_all):
    i = pl.program_id(1)
    layer = l_ref[0]
    n_lat = (k_ref.shape[0] - ctx_len) // tk
    lam = lam_ref[DIFF_TAB * layer]
    out_scale = lam_ref[DIFF_TAB * layer + 1]
    bound = lam_ref[DIFF_TAB * layer + 2]
    lane = lax.broadcasted_iota(jnp.int32, (1, HEAD_DIM), 1)
    in_map0 = (lane & 32) == 0
    q = q_ref[...]
    qs = (jnp.where(in_map0, q, jnp.zeros_like(q)), jnp.where(in_map0, jnp.zeros_like(q), q))

    def lat_chunk(c):
        off = pl.multiple_of(ctx_len + c * tk, math.gcd(ctx_len, tk))
        return k_ref[pl.ds(off, tk), :], v_ref[pl.ds(off, tk), :]

    def finish(l0, l1):
        o = acc_ref[0] / l0 - lam * (acc_ref[1] / l1)
        o = o * lax.rsqrt(jnp.mean(o * o, axis=-1, keepdims=True) + NORM_EPS) * sw_ref[0] * out_scale
        o_ref[...] = o.astype(o_ref.dtype)

    @pl.when(bound >= 0.0)
    def _():
        tq = q.shape[0]
        q_both = jnp.concatenate(qs, axis=0)

        def attend(n_keys, chunk):
            for lo in range(0, n_keys, chunk):
                p = jnp.exp2(_nt_dot(q_both, k_ref[lo:lo + chunk, :]) - bound)
                prt = p[:, 0:HEAD_DIM]
                for j in range(1, chunk // HEAD_DIM):
                    prt = prt + p[:, j * HEAD_DIM:(j + 1) * HEAD_DIM]
                pv = jnp.dot(p.astype(BF16), v_ref[lo:lo + chunk, :], preferred_element_type=F32)
                for mi in range(2):
                    if lo == 0:
                        part_ref[mi] = prt[mi * tq:(mi + 1) * tq]
                        acc_ref[mi] = pv[mi * tq:(mi + 1) * tq]
                    else:
                        part_ref[mi] += prt[mi * tq:(mi + 1) * tq]
                        acc_ref[mi] += pv[mi * tq:(mi + 1) * tq]

        @pl.when(i == 0)
        def _():
            attend(ctx_len, ctx_len)

        @pl.when(i > 0)
        def _():
            attend(k_ref.shape[0], tk_all)

        finish(jnp.sum(part_ref[0], axis=-1, keepdims=True),
               jnp.sum(part_ref[1], axis=-1, keepdims=True))

    @pl.when(bound < 0.0)
    def _():
        m_ref[...] = jnp.full(m_ref.shape, -jnp.inf, F32)
        s_ref[...] = jnp.zeros(s_ref.shape, F32)
        acc_ref[...] = jnp.zeros(acc_ref.shape, F32)

        def step(kc, vc):
            for mi in range(2):
                s = _nt_dot(qs[mi], kc)
                m_prev = m_ref[mi]
                m_new = jnp.maximum(m_prev, jnp.max(s, axis=-1, keepdims=True))
                alpha = jnp.exp2(m_prev - m_new)
                p = jnp.exp2(s - m_new)
                s_ref[mi] = alpha * s_ref[mi] + jnp.sum(p, axis=-1, keepdims=True)
                acc_ref[mi] = alpha * acc_ref[mi] + jnp.dot(p.astype(BF16), vc,
                                                           preferred_element_type=F32)
                m_ref[mi] = m_new

        step(k_ref[0:ctx_len, :], v_ref[0:ctx_len, :])

        def body(c, carry):
            step(*lat_chunk(c))
            return carry
        lax.fori_loop(0, jnp.where(i == 0, 0, n_lat), body, 0)
        finish(s_ref[0], s_ref[1])


def _diff_attn(larr, lam_tab, proj, subln, n_heads, ctx_len):
    t = proj.shape[0]
    depth = subln.shape[0]
    tq = ROW_TILE
    tk = _pick_tile(t - ctx_len, (DIFF_TK, 256))
    n_lat = (t - ctx_len) // tk
    return pl.pallas_call(
        functools.partial(_diff_attn_kernel, ctx_len=ctx_len, tk=tk,
                          tk_all=_pick_tile(t, DIFF_TK_ALL)),
        grid_spec=pltpu.PrefetchScalarGridSpec(
            num_scalar_prefetch=1, grid=(n_heads, t // tq),
            in_specs=[pl.BlockSpec(memory_space=pltpu.SMEM),
                      pl.BlockSpec((tq, HEAD_DIM), lambda h, i, l: (i, h)),
                      pl.BlockSpec((t, HEAD_DIM), lambda h, i, l: (0, n_heads + h)),
                      pl.BlockSpec((t, HEAD_DIM), lambda h, i, l: (0, 2 * n_heads + h)),
                      pl.BlockSpec((1, 1, HEAD_DIM), lambda h, i, l: (l[0], 0, 0))],
            out_specs=pl.BlockSpec((tq, HEAD_DIM), lambda h, i, l: (i, h)),
            scratch_shapes=[pltpu.VMEM((2, tq, 1), F32), pltpu.VMEM((2, tq, 1), F32),
                            pltpu.VMEM((2, tq, HEAD_DIM), F32),
                            pltpu.VMEM((2, tq, HEAD_DIM), F32)]),
        out_shape=jax.ShapeDtypeStruct((t, n_heads * HEAD_DIM), BF16),
        compiler_params=_cparams(("arbitrary", "arbitrary")),
        name="diff_attn",
    )(larr, lam_tab, proj, proj, proj, subln.reshape(depth, 1, HEAD_DIM))


def _na_kernel(l_ref, idx_ref, q_ref, kc_ref, k0_ref, k1_ref, k2_ref, vc_ref, v0_ref, v1_ref, v2_ref,
               slab_ref, o_ref, *, n_heads):
    kw = (k0_ref, k1_ref, k2_ref)
    vw = (v0_ref, v1_ref, v2_ref)
    g = pl.program_id(0)
    n_tiles = pl.num_programs(0)
    pat = jnp.where(g == 0, 0, jnp.where(g == 1, 1, jnp.where(g == n_tiles - 1, 3, 2)))
    n_kr = NA_WIN_TILES * NA_ROWS
    low_lanes = lax.broadcasted_iota(jnp.int32, (GRID_W, 2 * GRID_W), 1) < GRID_W

    def bias_tile(h, j):
        rows = []
        for a in range(NA_ROWS):
            pairs = []
            for b in range(NA_ROWS * j, NA_ROWS * (j + 1), 2):
                i0 = idx_ref[(pat * NA_ROWS + a) * n_kr + b]
                i1 = idx_ref[(pat * NA_ROWS + a) * n_kr + b + 1]
                pairs.append(jnp.where(low_lanes, slab_ref[0, h, i0], slab_ref[0, h, i1]))
            rows.append(jnp.concatenate(pairs, axis=1))
        return jnp.concatenate(rows, axis=0)

    for h in range(n_heads):
        hs = slice(h * HEAD_DIM, (h + 1) * HEAD_DIM)
        qh = q_ref[:, hs]
        s_c = _nt_dot(qh, kc_ref[:, hs])
        s_w = [_nt_dot(qh, kw[j][:, hs]) + bias_tile(h, j) for j in range(NA_WIN_TILES)]
        m = jnp.max(s_c, axis=-1, keepdims=True)
        for s in s_w:
            m = jnp.maximum(m, jnp.max(s, axis=-1, keepdims=True))
        p_c = jnp.exp2(s_c - m)
        denom = jnp.sum(p_c, axis=-1, keepdims=True)
        o = jnp.dot(p_c.astype(BF16), vc_ref[:, hs], preferred_element_type=F32)
        for j in range(NA_WIN_TILES):
            p = jnp.exp2(s_w[j] - m)
            denom = denom + jnp.sum(p, axis=-1, keepdims=True)
            o = o + jnp.dot(p.astype(BF16), vw[j][:, hs], preferred_element_type=F32)
        o_ref[:, hs] = (o / denom).astype(o_ref.dtype)


def _na_win_base(g, n_tiles):
    return 1 + jnp.clip(g - 2, 0, n_tiles - 1 - NA_WIN_TILES)


def _na_attn(larr, proj, slabs, slab_idx, n_heads, n_diff_heads):
    t = proj.shape[0]
    n_tiles = t // ROW_TILE
    w = n_heads * HEAD_DIM
    qcol = 3 * n_diff_heads * HEAD_DIM // w
    blk = (ROW_TILE, w)

    def win_spec(col, j):
        return pl.BlockSpec(blk, lambda g, l, ix: (_na_win_base(g, n_tiles) + j, col))

    in_specs = [pl.BlockSpec(blk, lambda g, l, ix: (g, qcol)),
                pl.BlockSpec(blk, lambda g, l, ix: (0, qcol + 1))]
    in_specs += [win_spec(qcol + 1, j) for j in range(NA_WIN_TILES)]
    in_specs += [pl.BlockSpec(blk, lambda g, l, ix: (0, qcol + 2))]
    in_specs += [win_spec(qcol + 2, j) for j in range(NA_WIN_TILES)]
    in_specs += [pl.BlockSpec((1,) + slabs.shape[1:], lambda g, l, ix: (l[0], 0, 0, 0, 0))]
    return pl.pallas_call(
        functools.partial(_na_kernel, n_heads=n_heads),
        grid_spec=pltpu.PrefetchScalarGridSpec(
            num_scalar_prefetch=2, grid=(n_tiles,),
            in_specs=in_specs,
            out_specs=pl.BlockSpec(blk, lambda g, l, ix: (g, 0))),
        out_shape=jax.ShapeDtypeStruct((t, w), BF16),
        compiler_params=_cparams(("arbitrary",)),
        name="na_attn",
    )(larr, slab_idx, *([proj] * 9), slabs)


def _na_bias_slabs(rpb, n_tiles):
    depth, n_h = rpb.shape[:2]
    rows = (n_tiles - 1) * NA_ROWS
    kr = min(WIN_R, rows)
    n_dr = 2 * WIN_R - 1
    n_kr = NA_WIN_TILES * NA_ROWS
    qc = np.arange(GRID_W)[:, None]
    kc = np.arange(GRID_W)[None, :]
    cs = np.clip(qc - WIN_C // 2, 0, GRID_W - WIN_C)
    col_valid = (kc >= cs) & (kc < cs + WIN_C)
    pad = GRID_W - WIN_C
    row = jnp.concatenate([jnp.repeat(rpb[..., :1], pad, axis=-1), rpb,
                           jnp.repeat(rpb[..., -1:], pad, axis=-1)], axis=-1)
    slabs = jnp.stack([row[..., GRID_W - 1 - c:2 * GRID_W - 1 - c] for c in range(GRID_W)], axis=-2)
    slabs = jnp.where(col_valid[None, None, None], slabs, NEG_INF) * LOG2E
    masked = jnp.full((depth, n_h, 1, GRID_W, GRID_W), NEG_INF * LOG2E, F32)
    slabs = jnp.concatenate([slabs, masked], axis=2)
    slab_idx = np.full((4, NA_ROWS, n_kr), n_dr, np.int32)
    for p, gl in enumerate((0, 1, n_tiles - 2)):
        qr = NA_ROWS * gl + np.arange(NA_ROWS)[:, None]
        win_start = NA_ROWS * int(np.clip(gl - 1, 0, n_tiles - 1 - NA_WIN_TILES))
        krow = win_start + np.arange(n_kr)[None, :]
        rs = np.clip(qr - kr // 2, 0, rows - kr)
        row_valid = (krow >= rs) & (krow < rs + kr)
        slab_idx[p + 1] = np.where(row_valid, krow - qr + (WIN_R - 1), n_dr)
    slabs = jnp.concatenate([slabs, slabs], axis=-1)
    return slabs, jnp.asarray(slab_idx.reshape(-1))


def _out_router_kernel(l_ref, d_ref, n_ref, wa_ref, wb_ref, x_ref, g1_ref, nw_ref, sh_ref, sc_ref,
                       wr_ref, xo_ref, h_ref, ids_ref, gates_ref):
    is_ctx = pl.program_id(0) == 0
    acc = jnp.dot(d_ref[...], wa_ref[0], preferred_element_type=F32)
    acc = acc + jnp.dot(n_ref[...], wb_ref[0], preferred_element_type=F32)
    x = x_ref[...] + _mod_row(g1_ref, is_ctx) * acc
    xo_ref[...] = x
    y = x * lax.rsqrt(jnp.mean(x * x, axis=-1, keepdims=True) + NORM_EPS) * nw_ref[0]
    y = y * (1.0 + _mod_row(sc_ref, is_ctx)) + _mod_row(sh_ref, is_ctx)
    h_ref[...] = y

    y_hi = y.astype(BF16)
    y_lo = (y - y_hi.astype(F32)).astype(BF16)
    lg = jnp.dot(y_hi, wr_ref[0, 0], preferred_element_type=F32)
    lg = lg + (jnp.dot(y_lo, wr_ref[0, 0], preferred_element_type=F32)
               + jnp.dot(y_hi, wr_ref[0, 1], preferred_element_type=F32))

    lane = lax.broadcasted_iota(jnp.int32, lg.shape, 1).astype(F32)
    big = jnp.float32(1e9)
    is_g = (lane >= N_EXPERTS) & (lane < N_EXPERTS + N_GROUPS)
    gl = jnp.where(is_g, lg, -jnp.inf)
    gmax = jnp.max(gl, axis=-1, keepdims=True)
    gsel = jnp.min(jnp.where(gl == gmax, lane, big), axis=-1, keepdims=True) - N_EXPERTS
    g_gate = 1.0 / jnp.sum(jnp.where(is_g, jnp.exp(gl - gmax), 0.0), axis=-1, keepdims=True)
    lo = gsel * EXPERTS_PER_GROUP
    el = jnp.where((lane >= lo) & (lane < lo + EXPERTS_PER_GROUP), lg, -jnp.inf)
    e1 = jnp.max(el, axis=-1, keepdims=True)
    i1 = jnp.min(jnp.where(el == e1, lane, big), axis=-1, keepdims=True)
    el2 = jnp.where(lane == i1, -jnp.inf, el)
    e2 = jnp.max(el2, axis=-1, keepdims=True)
    i2 = jnp.min(jnp.where(el2 == e2, lane, big), axis=-1, keepdims=True)
    r = jnp.exp(e2 - e1)
    w1 = 1.0 / (1.0 + r)
    w2 = r / (1.0 + r)
    ids_ref[...] = jnp.where(lane == 0, i1, jnp.where(lane == 1, i2, 0.0)).astype(jnp.int32)
    gates_ref[...] = jnp.where(lane == 0, g_gate * w1, jnp.where(lane == 1, g_gate * w2, 0.0))


def _out_router(larr, d_out, n_out, w_out_b, xa, mods, norm2_w, wr):
    t, d = xa.shape
    half = d_out.shape[1]
    depth = norm2_w.shape[0]
    tile = lambda width: pl.BlockSpec((ROW_TILE, width), lambda i, l: (i, 0))
    return pl.pallas_call(
        _out_router_kernel,
        grid_spec=pltpu.PrefetchScalarGridSpec(
            num_scalar_prefetch=1, grid=(t // ROW_TILE,),
            in_specs=[tile(half), tile(half),
                      pl.BlockSpec((1, half, d), lambda i, l: (l[0], 0, 0)),
                      pl.BlockSpec((1, half, d), lambda i, l: (l[0], 1, 0)),
                      tile(d), _mod_spec(d, 2),
                      pl.BlockSpec((1, 1, d), lambda i, l: (l[0], 0, 0)),
                      _mod_spec(d, 3), _mod_spec(d, 4),
                      pl.BlockSpec((1, 2, d, HEAD_DIM), lambda i, l: (l[0], 0, 0, 0))],
            out_specs=[tile(d), tile(d), tile(HEAD_DIM), tile(HEAD_DIM)]),
        out_shape=[jax.ShapeDtypeStruct((t, d), F32), jax.ShapeDtypeStruct((t, d), F32),
                   jax.ShapeDtypeStruct((t, HEAD_DIM), jnp.int32),
                   jax.ShapeDtypeStruct((t, HEAD_DIM), F32)],
        compiler_params=_cparams(("arbitrary",)),
        name="out_router",
    )(larr, d_out, n_out, w_out_b, w_out_b, xa, mods, norm2_w.reshape(depth, 1, d), mods, mods, wr)


def _expert_kernel(l_ref, be_ref, st_ref, nu_ref, h_hbm, w1_ref, w3_ref, w2_ref, y_ref,
                   xbuf0, xbuf1, wb1, wb3, wb2, sem):
    b = pl.program_id(0)
    n_blocks = pl.num_programs(0)
    xbufs = (xbuf0, xbuf1)
    n_chunks = MOE_BLOCK // GATHER_CHUNK

    def rows_of(blk):
        return jnp.where(blk < n_blocks, nu_ref[jnp.minimum(blk, n_blocks - 1)], 0)

    def gather(blk, slot):
        n_rows = rows_of(blk)
        for c in range(n_chunks):
            @pl.when(c * GATHER_CHUNK < n_rows)
            def _():
                for r in range(c * GATHER_CHUNK, (c + 1) * GATHER_CHUNK):
                    tok = st_ref[blk * MOE_BLOCK + r]
                    pltpu.make_async_copy(h_hbm.at[pl.ds(tok, 1), :],
                                          xbufs[slot].at[pl.ds(r, 1), :], sem.at[slot]).start()

    def wait_rows(blk, slot):
        n_rows = rows_of(blk)
        for c in range(n_chunks):
            @pl.when(c * GATHER_CHUNK < n_rows)
            def _():
                pltpu.make_async_copy(h_hbm.at[pl.ds(0, GATHER_CHUNK), :],
                                      xbufs[slot].at[pl.ds(c * GATHER_CHUNK, GATHER_CHUNK), :],
                                      sem.at[slot]).wait()

    @pl.when(b == 0)
    def _():
        xbuf0[...] = jnp.zeros(xbuf0.shape, F32)
        xbuf1[...] = jnp.zeros(xbuf1.shape, F32)
        gather(0, 0)

    used = rows_of(b) > 0

    @pl.when(used & ((b == 0) | (be_ref[b] != be_ref[jnp.maximum(b - 1, 0)])))
    def _():
        wb1[...] = w1_ref[0, 0].astype(BF16)
        wb3[...] = w3_ref[0, 0].astype(BF16)
        wb2[...] = w2_ref[0, 0].astype(BF16)

    def run(slot):
        wait_rows(b, slot)
        gather(b + 1, 1 - slot)
        xb = xbufs[slot][...].astype(BF16)
        a = jnp.dot(xb, wb1[...], preferred_element_type=F32)
        g = jnp.dot(xb, wb3[...], preferred_element_type=F32)
        mid = (a / (1.0 + jnp.exp(-a))) * g
        y_ref[...] = jnp.dot(mid.astype(BF16), wb2[...], preferred_element_type=F32)

    for slot in range(2):
        pl.when(used & (b % 2 == slot))(functools.partial(run, slot))

    @pl.when(jnp.logical_not(used))
    def _():
        y_ref[...] = jnp.zeros(y_ref.shape, y_ref.dtype)


def _experts(larr, block_expert, slot_tok, block_rows, h2, w1, w3, w2):
    t, d = h2.shape
    ff = w1.shape[-1]
    n_blocks = block_expert.shape[0]
    return pl.pallas_call(
        _expert_kernel,
        grid_spec=pltpu.PrefetchScalarGridSpec(
            num_scalar_prefetch=4, grid=(n_blocks,),
            in_specs=[pl.BlockSpec(memory_space=pl.ANY),
                      pl.BlockSpec((1, 1, d, ff), lambda b, l, be, st, nu: (l[0], be[b], 0, 0)),
                      pl.BlockSpec((1, 1, d, ff), lambda b, l, be, st, nu: (l[0], be[b], 0, 0)),
                      pl.BlockSpec((1, 1, ff, d), lambda b, l, be, st, nu: (l[0], be[b], 0, 0))],
            out_specs=pl.BlockSpec((MOE_BLOCK, d), lambda b, l, be, st, nu: (b, 0)),
            scratch_shapes=[pltpu.VMEM((MOE_BLOCK, d), F32), pltpu.VMEM((MOE_BLOCK, d), F32),
                            pltpu.VMEM((d, ff), BF16), pltpu.VMEM((d, ff), BF16),
                            pltpu.VMEM((ff, d), BF16),
                            pltpu.SemaphoreType.DMA((2,))]),
        out_shape=jax.ShapeDtypeStruct((n_blocks * MOE_BLOCK, d), F32),
        compiler_params=_cparams(("arbitrary",)),
        name="experts",
    )(larr, block_expert, slot_tok, block_rows, h2, w1, w3, w2)


def _combine_kernel(l_ref, dest_ref, y_hbm, x_ref, gates_ref, g2_ref, nw_ref, sh_ref, sc_ref,
                    o_ref, h_ref, ybuf, sem):
    i = pl.program_id(0)
    is_ctx = i == 0
    buf = i % 2

    def gather(tile, dst):
        def issue(r, carry):
            for k in range(2):
                slot = dest_ref[(tile * ROW_TILE + r) * 2 + k]
                pltpu.make_async_copy(y_hbm.at[pl.ds(slot, 1), :],
                                      ybuf.at[dst, pl.ds(k * ROW_TILE + r, 1), :],
                                      sem.at[dst]).start()
            return carry
        lax.fori_loop(0, ROW_TILE, issue, 0)

    @pl.when(i == 0)
    def _():
        gather(0, 0)

    pltpu.make_async_copy(y_hbm.at[pl.ds(0, 2 * ROW_TILE), :], ybuf.at[buf], sem.at[buf]).wait()

    @pl.when(i + 1 < pl.num_programs(0))
    def _():
        gather(i + 1, 1 - buf)

    gates = gates_ref[...]
    y = (gates[:, 0:1] * ybuf[buf, 0:ROW_TILE, :]
         + gates[:, 1:2] * ybuf[buf, ROW_TILE:2 * ROW_TILE, :])
    x = x_ref[...] + _mod_row(g2_ref, is_ctx) * y
    o_ref[...] = x
    h = x * lax.rsqrt(jnp.mean(x * x, axis=-1, keepdims=True) + NORM_EPS) * nw_ref[0]
    h = h * (1.0 + _mod_row(sc_ref, is_ctx)) + _mod_row(sh_ref, is_ctx)
    h_ref[...] = h.astype(h_ref.dtype)


def _combine(larr, dest, y_slots, x_new, gates, mods, norm1_w):
    t, d = x_new.shape
    depth = norm1_w.shape[0]
    nxt = lambda l: jnp.minimum(l[0] + 1, depth - 1)
    tile = pl.BlockSpec((ROW_TILE, d), lambda i, l, dd: (i, 0))
    return pl.pallas_call(
        _combine_kernel,
        grid_spec=pltpu.PrefetchScalarGridSpec(
            num_scalar_prefetch=2, grid=(t // ROW_TILE,),
            in_specs=[pl.BlockSpec(memory_space=pl.ANY),
                      tile,
                      pl.BlockSpec((ROW_TILE, HEAD_DIM), lambda i, l, dd: (i, 0)),
                      pl.BlockSpec((1, 8, d), lambda i, l, dd: (l[0], 0, 5)),
                      pl.BlockSpec((1, 1, d), lambda i, l, dd: (nxt(l), 0, 0)),
                      pl.BlockSpec((1, 8, d), lambda i, l, dd: (nxt(l), 0, 0)),
                      pl.BlockSpec((1, 8, d), lambda i, l, dd: (nxt(l), 0, 1))],
            out_specs=[tile, tile],
            scratch_shapes=[pltpu.VMEM((2, 2 * ROW_TILE, d), F32),
                            pltpu.SemaphoreType.DMA((2,))]),
        out_shape=[jax.ShapeDtypeStruct((t, d), F32), jax.ShapeDtypeStruct((t, d), BF16)],
        compiler_params=_cparams(("arbitrary",)),
        name="combine",
    )(larr, dest, y_slots, x_new, gates, mods, norm1_w.reshape(depth, 1, d), mods, mods)


def _dispatch(ids):
    t = ids.shape[0]
    n_assign = 2 * t
    e_flat = ids.reshape(n_assign)
    onehot = (e_flat[:, None] == jnp.arange(N_EXPERTS, dtype=jnp.int32)[None, :]).astype(jnp.int32)
    csum = jnp.cumsum(onehot, axis=0)
    rank = jnp.sum(onehot * (csum - 1), axis=1)
    counts = csum[-1]
    padded = (counts + MOE_BLOCK - 1) // MOE_BLOCK * MOE_BLOCK
    padded_end = jnp.cumsum(padded)
    padded_start = padded_end - padded
    dest = (padded_start[e_flat] + rank).astype(jnp.int32)
    n_blocks = -(-n_assign // MOE_BLOCK) + N_EXPERTS
    tok = jnp.arange(n_assign, dtype=jnp.int32) // 2
    slot_tok = jnp.zeros((n_blocks * MOE_BLOCK,), jnp.int32).at[dest].set(tok)
    block_start = jnp.arange(n_blocks, dtype=jnp.int32) * MOE_BLOCK
    block_expert = jnp.clip(jnp.searchsorted(padded_end, block_start, side='right'),
                            0, N_EXPERTS - 1).astype(jnp.int32)
    filled_end = (padded_start + counts)[block_expert]
    block_rows = jnp.clip(filled_end - block_start, 0, MOE_BLOCK).astype(jnp.int32)
    return dest, slot_tok, block_expert, block_rows


def _diff_lane_fields():
    lane = np.arange(HEAD_DIM)
    part, m, half, f = lane // 64, (lane % 64) // 32, (lane % 32) // 16, lane % 16
    return part, m, half, f


def _rope_tables(ctx_len, seq):
    part, m, half, f = _diff_lane_fields()
    nf = HEAD_DIM // 8
    inv = ROPE_BASE ** (-jnp.arange(nf, dtype=F32) / nf)
    pos = jnp.arange(seq)
    prow = (pos // GRID_W).astype(F32)
    pcol = (pos % GRID_W).astype(F32)
    p = jnp.where(jnp.asarray(half)[None, :] == 0, prow[:, None], pcol[:, None])
    ang = p * inv[jnp.asarray(f)][None, :]
    sign = jnp.where(jnp.asarray(part) == 0, -1.0, 1.0).astype(F32)[None, :]
    cos = jnp.concatenate([jnp.ones((ctx_len, HEAD_DIM), F32), jnp.cos(ang)], axis=0)
    sin = jnp.concatenate([jnp.zeros((ctx_len, HEAD_DIM), F32), jnp.sin(ang) * sign], axis=0)
    return cos, sin


def _permute_diff_cols(w, n_heads):
    lead = w.shape[:-1]
    w = w.reshape(lead + (n_heads, 2, 2, 2, 16))
    nd = len(lead)
    w = jnp.transpose(w, tuple(range(nd)) + (nd, nd + 3, nd + 1, nd + 2, nd + 4))
    return w.reshape(lead + (n_heads * HEAD_DIM,))


def _qk_weight_table(diff_q_norm, diff_k_norm, na_q_norm, na_k_norm):
    part, m, half, f = _diff_lane_fields()
    src = half * 32 + part * 16 + f
    dqk = HEAD_DIM // 2
    dq = diff_q_norm[:, src] * (dqk ** -0.5 * LOG2E)
    dk = diff_k_norm[:, src]
    nq = na_q_norm * (HEAD_DIM ** -0.5 * LOG2E)
    rows = jnp.stack([dq, dk, nq, na_k_norm], axis=1)
    return jnp.concatenate([rows, jnp.zeros_like(rows)], axis=1)


def kernel(x, c, ctx, c_ctx, ada_w, ada_b, norm1_w, norm2_w, w_in, w_out, diff_q_norm, diff_k_norm,
           diff_lq1, diff_lk1, diff_lq2, diff_lk2, diff_subln, na_q_norm, na_k_norm, na_rpb,
           moe_w_group, moe_w_expert, moe_w1, moe_w3, moe_w2):
    _, seq, d = x.shape
    ctx_len = ctx.shape[1]
    depth = ada_w.shape[0]
    assert ctx_len == ROW_TILE and seq % ROW_TILE == 0 and x.shape[0] == 1
    n_heads = d // HEAD_DIM
    n_diff = n_heads // 2
    n_na = n_heads - n_diff
    seg = n_diff * HEAD_DIM
    t = ctx_len + seq
    n_tiles = t // ROW_TILE

    xa = jnp.concatenate([ctx[0], x[0]], axis=0)
    c2 = jnp.zeros((8, d), F32).at[0].set(c[0]).at[1].set(c_ctx)
    mods = _adaln(c2, ada_w, ada_b)

    w_in_b = w_in.astype(BF16)
    w_q = _permute_diff_cols(w_in_b[..., :seg], n_diff)
    w_k = _permute_diff_cols(w_in_b[..., seg:2 * seg], n_diff)
    w_out_b = w_out.astype(BF16)
    qkw = _qk_weight_table(diff_q_norm, diff_k_norm, na_q_norm, na_k_norm)
    cos_t, sin_t = _rope_tables(ctx_len, seq)
    na_slabs, na_slab_idx = _na_bias_slabs(na_rpb, n_tiles)

    lam_init = jnp.asarray([0.8 - 0.6 * math.exp(-0.3 * l) for l in range(depth)], F32)
    lam = (jnp.exp(jnp.sum(diff_lq1 * diff_lk1, axis=-1)) - jnp.exp(jnp.sum(diff_lq2 * diff_lk2, axis=-1))
           + lam_init)
    bound = (HEAD_DIM // 2) * jnp.max(jnp.abs(qkw[:, 0]), axis=-1) * jnp.max(jnp.abs(qkw[:, 1]), axis=-1)
    bound = bound * 1.02 + 0.01
    bound = jnp.where(bound <= MAX_STATIC_SHIFT, bound, -1.0)
    lam_tab = jnp.stack([lam, 1.0 - lam_init, bound], axis=1).reshape(DIFF_TAB * depth)

    wr = jnp.concatenate([moe_w_expert, moe_w_group,
                          jnp.zeros((depth, d, HEAD_DIM - N_EXPERTS - N_GROUPS), F32)], axis=-1)
    wr_hi = wr.astype(BF16)
    wr_lo = (wr - wr_hi.astype(F32)).astype(BF16)
    wr_split = jnp.stack([wr_hi, wr_lo], axis=1)

    h = _norm_mod(jnp.zeros((1,), jnp.int32), xa, norm1_w, mods)
    for l in range(depth):
        larr = jnp.full((1,), l, jnp.int32)
        proj = _in_proj(larr, h, w_q, w_k, w_in_b, qkw, cos_t, sin_t)
        d_out = _diff_attn(larr, lam_tab, proj, diff_subln, n_diff, ctx_len)
        n_out = _na_attn(larr, proj, na_slabs, na_slab_idx, n_na, n_diff)
        x_new, h2, ids, gates = _out_router(larr, d_out, n_out, w_out_b, xa, mods, norm2_w, wr_split)
        dest, slot_tok, block_expert, block_rows = _dispatch(ids[:, :2])
        y_slots = _experts(larr, block_expert, slot_tok, block_rows, h2, moe_w1, moe_w3, moe_w2)
        xa, h = _combine(larr, dest, y_slots, x_new, gates, mods, norm1_w)
    return xa[ctx_len:][None]
```

```python
import functools
import math

import numpy as np
import jax
import jax.numpy as jnp
from jax import lax
from jax.experimental import pallas as pl
from jax.experimental.pallas import tpu as pltpu

F32 = jnp.float32
BF16 = jnp.bfloat16

GRID_W = 64
HEAD_DIM = 128
WIN_R = 8
WIN_C = 16
ROPE_BASE = 10000.0
N_GROUPS = 4
EXPERTS_PER_GROUP = 8
N_EXPERTS = N_GROUPS * EXPERTS_PER_GROUP
NORM_EPS = 1e-6
NEG_INF = -1e30
LOG2E = 1.4426950408889634

ROW_TILE = 256
NA_ROWS = ROW_TILE // GRID_W
NA_WIN_TILES = 3
MOE_BLOCK = 256
GATHER_CHUNK = 32
ADA_TN = 1536
DIFF_TK = 512
DIFF_TK_ALL = (1408, 768, 640, 512, 256)
VMEM_LIMIT = 56 * 1024 * 1024


def _cparams(sem, vmem=VMEM_LIMIT):
    return pltpu.CompilerParams(dimension_semantics=sem, vmem_limit_bytes=vmem)


def _nt_dot(a, b):
    return lax.dot_general(a, b, (((1,), (1,)), ((), ())), preferred_element_type=F32)


def _adaln_kernel(c_ref, w_ref, b_ref, o_ref):
    c = c_ref[...]
    a = c / (1.0 + jnp.exp(-c))
    o_ref[0] = jnp.dot(a.astype(BF16), w_ref[0].astype(BF16),
                       preferred_element_type=F32) + b_ref[0]


def _adaln(c2, ada_w, ada_b):
    depth, d, six_d = ada_w.shape
    tn = ADA_TN
    return pl.pallas_call(
        _adaln_kernel,
        grid=(depth, six_d // tn),
        in_specs=[pl.BlockSpec((8, d), lambda l, j: (0, 0)),
                  pl.BlockSpec((1, d, tn), lambda l, j: (l, 0, j)),
                  pl.BlockSpec((1, 1, tn), lambda l, j: (l, 0, j))],
        out_specs=pl.BlockSpec((1, 8, tn), lambda l, j: (l, 0, j)),
        out_shape=jax.ShapeDtypeStruct((depth, 8, six_d), F32),
        compiler_params=_cparams(("arbitrary", "arbitrary")),
        name="adaln",
    )(c2, ada_w, ada_b.reshape(depth, 1, six_d))


def _mod_spec(d, chunk):
    return pl.BlockSpec((1, 8, d), lambda i, l: (l[0], 0, chunk))


def _mod_row(ref, is_ctx):
    return jnp.where(is_ctx, ref[0, 1:2, :], ref[0, 0:1, :])


def _norm_mod_kernel(l_ref, x_ref, nw_ref, sh_ref, sc_ref, o_ref):
    is_ctx = pl.program_id(0) == 0
    x = x_ref[...]
    y = x * lax.rsqrt(jnp.mean(x * x, axis=-1, keepdims=True) + NORM_EPS) * nw_ref[0]
    y = y * (1.0 + _mod_row(sc_ref, is_ctx)) + _mod_row(sh_ref, is_ctx)
    o_ref[...] = y.astype(o_ref.dtype)


def _norm_mod(larr, xa, norm_w, mods):
    t, d = xa.shape
    depth = norm_w.shape[0]
    return pl.pallas_call(
        _norm_mod_kernel,
        grid_spec=pltpu.PrefetchScalarGridSpec(
            num_scalar_prefetch=1, grid=(t // ROW_TILE,),
            in_specs=[pl.BlockSpec((ROW_TILE, d), lambda i, l: (i, 0)),
                      pl.BlockSpec((1, 1, d), lambda i, l: (l[0], 0, 0)),
                      _mod_spec(d, 0), _mod_spec(d, 1)],
            out_specs=pl.BlockSpec((ROW_TILE, d), lambda i, l: (i, 0))),
        out_shape=jax.ShapeDtypeStruct((t, d), BF16),
        compiler_params=_cparams(("arbitrary",)),
        name="norm_mod",
    )(larr, xa, norm_w.reshape(depth, 1, d), mods, mods)


def _in_proj_kernel(l_ref, x_ref, wq_ref, wk_ref, w_ref, qkw_ref, cos_ref, sin_ref, o_ref, acc_ref,
                    *, seg):
    j = pl.program_id(0)
    n_chunks = seg // HEAD_DIM

    for which, ref in ((j == 0, wq_ref), (j == 1, wk_ref), (j >= 2, w_ref)):
        @pl.when(which)
        def _(ref=ref):
            acc_ref[...] = jnp.dot(x_ref[...], ref[0], preferred_element_type=F32)

    @pl.when((j == 2) | (j == 5))
    def _():
        o_ref[...] = acc_ref[...].astype(o_ref.dtype)

    def normed(c, group_mat, inv_n, wrow):
        xc = acc_ref[:, c * HEAD_DIM:(c + 1) * HEAD_DIM]
        ss = jnp.dot((xc * xc).astype(BF16), group_mat, preferred_element_type=F32)
        return xc * lax.rsqrt(ss * inv_n + NORM_EPS) * wrow

    @pl.when(j < 2)
    def _():
        a = lax.broadcasted_iota(jnp.int32, (HEAD_DIM, HEAD_DIM), 0)
        b = lax.broadcasted_iota(jnp.int32, (HEAD_DIM, HEAD_DIM), 1)
        group_mat = ((a & 32) == (b & 32)).astype(BF16)
        wrow = jnp.where(j == 0, qkw_ref[0, 0:1, :], qkw_ref[0, 1:2, :])
        cos = cos_ref[...]
        sin = sin_ref[...]
        for c in range(n_chunks):
            y = normed(c, group_mat, 2.0 / HEAD_DIM, wrow)
            y = y * cos + pltpu.roll(y, HEAD_DIM // 2, 1) * sin
            o_ref[:, c * HEAD_DIM:(c + 1) * HEAD_DIM] = y.astype(o_ref.dtype)

    @pl.when((j == 3) | (j == 4))
    def _():
        group_mat = jnp.ones((HEAD_DIM, HEAD_DIM), BF16)
        wrow = jnp.where(j == 3, qkw_ref[0, 2:3, :], qkw_ref[0, 3:4, :])
        for c in range(n_chunks):
            y = normed(c, group_mat, 1.0 / HEAD_DIM, wrow)
            o_ref[:, c * HEAD_DIM:(c + 1) * HEAD_DIM] = y.astype(o_ref.dtype)


def _pick_tile(n, candidates):
    for c in candidates:
        if n % c == 0:
            return c
    raise ValueError(f"no tile for {n}")


def _in_proj(larr, h, w_q, w_k, w_in_b, qkw, cos_t, sin_t):
    t, d = h.shape
    seg = w_q.shape[2]
    tm = _pick_tile(t, (768, 256))
    return pl.pallas_call(
        functools.partial(_in_proj_kernel, seg=seg),
        grid_spec=pltpu.PrefetchScalarGridSpec(
            num_scalar_prefetch=1, grid=(6, t // tm),
            in_specs=[pl.BlockSpec((tm, d), lambda j, i, l: (i, 0)),
                      pl.BlockSpec((1, d, seg), lambda j, i, l: (l[0], 0, 0)),
                      pl.BlockSpec((1, d, seg), lambda j, i, l: (l[0], 0, 0)),
                      pl.BlockSpec((1, d, seg), lambda j, i, l: (l[0], 0, jnp.maximum(j, 2))),
                      pl.BlockSpec((1, 8, HEAD_DIM), lambda j, i, l: (l[0], 0, 0)),
                      pl.BlockSpec((tm, HEAD_DIM), lambda j, i, l: (i, 0)),
                      pl.BlockSpec((tm, HEAD_DIM), lambda j, i, l: (i, 0))],
            out_specs=pl.BlockSpec((tm, seg), lambda j, i, l: (i, j)),
            scratch_shapes=[pltpu.VMEM((tm, seg), F32)]),
        out_shape=jax.ShapeDtypeStruct((t, 6 * seg), BF16),
        compiler_params=_cparams(("arbitrary", "arbitrary")),
        name="in_proj",
    )(larr, h, w_q, w_k, w_in_b, qkw, cos_t, sin_t)


DIFF_HEADS_PER_STEP = (4, 2, 1)
DIFF_TAB = 3
MAX_STATIC_SHIFT = 60.0


def _diff_attn_kernel(l_ref, lam_ref, q_ref, k_ref, v_ref, sw_ref, o_ref, m_ref, s_ref, part_ref,
                      acc_ref, *, ctx_len, tk, tk_all, heads):
    i = pl.program_id(1)
    layer = l_ref[0]
    tq = q_ref.shape[0]
    n_lat = (k_ref.shape[0] - ctx_len) // tk
    lam = lam_ref[DIFF_TAB * layer]
    out_scale = lam_ref[DIFF_TAB * layer + 1]
    bound = lam_ref[DIFF_TAB * layer + 2]
    lane = lax.broadcasted_iota(jnp.int32, (1, HEAD_DIM), 1)
    in_map0 = (lane & 32) == 0

    def head_cols(hh):
        return slice(hh * HEAD_DIM, (hh + 1) * HEAD_DIM)

    def masked_q(hh):
        q = q_ref[:, head_cols(hh)]
        return (jnp.where(in_map0, q, jnp.zeros_like(q)), jnp.where(in_map0, jnp.zeros_like(q), q))

    def finish(hh, l0, l1):
        o = acc_ref[hh, 0] / l0 - lam * (acc_ref[hh, 1] / l1)
        o = o * lax.rsqrt(jnp.mean(o * o, axis=-1, keepdims=True) + NORM_EPS) * sw_ref[0] * out_scale
        o_ref[:, head_cols(hh)] = o.astype(o_ref.dtype)

    @pl.when(bound >= 0.0)
    def _():
        def attend(hh, n_keys, chunk):
            hs = head_cols(hh)
            q_both = jnp.concatenate(masked_q(hh), axis=0)
            for lo in range(0, n_keys, chunk):
                p = jnp.exp2(_nt_dot(q_both, k_ref[lo:lo + chunk, hs]) - bound)
                prt = p[:, 0:HEAD_DIM]
                for j in range(1, chunk // HEAD_DIM):
                    prt = prt + p[:, j * HEAD_DIM:(j + 1) * HEAD_DIM]
                pv = jnp.dot(p.astype(BF16), v_ref[lo:lo + chunk, hs], preferred_element_type=F32)
                for mi in range(2):
                    if lo == 0:
                        part_ref[hh, mi] = prt[mi * tq:(mi + 1) * tq]
                        acc_ref[hh, mi] = pv[mi * tq:(mi + 1) * tq]
                    else:
                        part_ref[hh, mi] += prt[mi * tq:(mi + 1) * tq]
                        acc_ref[hh, mi] += pv[mi * tq:(mi + 1) * tq]

        def run(n_keys, chunk):
            for hh in range(heads):
                attend(hh, n_keys, chunk)
                finish(hh, jnp.sum(part_ref[hh, 0], axis=-1, keepdims=True),
                       jnp.sum(part_ref[hh, 1], axis=-1, keepdims=True))

        @pl.when(i == 0)
        def _():
            run(ctx_len, ctx_len)

        @pl.when(i > 0)
        def _():
            run(k_ref.shape[0], tk_all)

    @pl.when(bound < 0.0)
    def _():
        for hh in range(heads):
            hs = head_cols(hh)
            qs = masked_q(hh)
            m_ref[...] = jnp.full(m_ref.shape, -jnp.inf, F32)
            s_ref[...] = jnp.zeros(s_ref.shape, F32)
            acc_ref[hh] = jnp.zeros(acc_ref.shape[1:], F32)

            def step(kc, vc, hh=hh, qs=qs):
                for mi in range(2):
                    s = _nt_dot(qs[mi], kc)
                    m_prev = m_ref[mi]
                    m_new = jnp.maximum(m_prev, jnp.max(s, axis=-1, keepdims=True))
                    alpha = jnp.exp2(m_prev - m_new)
                    p = jnp.exp2(s - m_new)
                    s_ref[mi] = alpha * s_ref[mi] + jnp.sum(p, axis=-1, keepdims=True)
                    acc_ref[hh, mi] = alpha * acc_ref[hh, mi] + jnp.dot(
                        p.astype(BF16), vc, preferred_element_type=F32)
                    m_ref[mi] = m_new

            step(k_ref[0:ctx_len, hs], v_ref[0:ctx_len, hs])

            def body(c, carry, step=step, hs=hs):
                off = pl.multiple_of(ctx_len + c * tk, math.gcd(ctx_len, tk))
                step(k_ref[pl.ds(off, tk), hs], v_ref[pl.ds(off, tk), hs])
                return carry
            lax.fori_loop(0, jnp.where(i == 0, 0, n_lat), body, 0)
            finish(hh, s_ref[0], s_ref[1])


def _diff_attn(larr, lam_tab, proj, subln, n_heads, ctx_len):
    t = proj.shape[0]
    depth = subln.shape[0]
    tq = ROW_TILE
    tk = _pick_tile(t - ctx_len, (DIFF_TK, 256))
    heads = _pick_tile(n_heads, DIFF_HEADS_PER_STEP)
    n_hb = n_heads // heads
    wb = heads * HEAD_DIM
    return pl.pallas_call(
        functools.partial(_diff_attn_kernel, ctx_len=ctx_len, tk=tk,
                          tk_all=_pick_tile(t, DIFF_TK_ALL), heads=heads),
        grid_spec=pltpu.PrefetchScalarGridSpec(
            num_scalar_prefetch=1, grid=(n_hb, t // tq),
            in_specs=[pl.BlockSpec(memory_space=pltpu.SMEM),
                      pl.BlockSpec((tq, wb), lambda h, i, l: (i, h)),
                      pl.BlockSpec((t, wb), lambda h, i, l: (0, n_hb + h)),
                      pl.BlockSpec((t, wb), lambda h, i, l: (0, 2 * n_hb + h)),
                      pl.BlockSpec((1, 1, HEAD_DIM), lambda h, i, l: (l[0], 0, 0))],
            out_specs=pl.BlockSpec((tq, wb), lambda h, i, l: (i, h)),
            scratch_shapes=[pltpu.VMEM((2, tq, 1), F32), pltpu.VMEM((2, tq, 1), F32),
                            pltpu.VMEM((heads, 2, tq, HEAD_DIM), F32),
                            pltpu.VMEM((heads, 2, tq, HEAD_DIM), F32)]),
        out_shape=jax.ShapeDtypeStruct((t, n_heads * HEAD_DIM), BF16),
        compiler_params=_cparams(("arbitrary", "arbitrary")),
        name="diff_attn",
    )(larr, lam_tab, proj, proj, proj, subln.reshape(depth, 1, HEAD_DIM))


def _na_kernel(l_ref, idx_ref, q_ref, kc_ref, k0_ref, k1_ref, k2_ref, vc_ref, v0_ref, v1_ref, v2_ref,
               slab_ref, o_ref, *, n_heads):
    kw = (k0_ref, k1_ref, k2_ref)
    vw = (v0_ref, v1_ref, v2_ref)
    g = pl.program_id(0)
    n_tiles = pl.num_programs(0)
    pat = jnp.where(g == 0, 0, jnp.where(g == 1, 1, jnp.where(g == n_tiles - 1, 3, 2)))
    n_kr = NA_WIN_TILES * NA_ROWS
    low_lanes = lax.broadcasted_iota(jnp.int32, (GRID_W, 2 * GRID_W), 1) < GRID_W

    def bias_tile(h, j):
        rows = []
        for a in range(NA_ROWS):
            pairs = []
            for b in range(NA_ROWS * j, NA_ROWS * (j + 1), 2):
                i0 = idx_ref[(pat * NA_ROWS + a) * n_kr + b]
                i1 = idx_ref[(pat * NA_ROWS + a) * n_kr + b + 1]
                pairs.append(jnp.where(low_lanes, slab_ref[0, h, i0], slab_ref[0, h, i1]))
            rows.append(jnp.concatenate(pairs, axis=1))
        return jnp.concatenate(rows, axis=0)

    for h in range(n_heads):
        hs = slice(h * HEAD_DIM, (h + 1) * HEAD_DIM)
        qh = q_ref[:, hs]
        s_c = _nt_dot(qh, kc_ref[:, hs])
        s_w = [_nt_dot(qh, kw[j][:, hs]) + bias_tile(h, j) for j in range(NA_WIN_TILES)]
        m = jnp.max(s_c, axis=-1, keepdims=True)
        for s in s_w:
            m = jnp.maximum(m, jnp.max(s, axis=-1, keepdims=True))
        p_c = jnp.exp2(s_c - m)
        denom = jnp.sum(p_c, axis=-1, keepdims=True)
        o = jnp.dot(p_c.astype(BF16), vc_ref[:, hs], preferred_element_type=F32)
        for j in range(NA_WIN_TILES):
            p = jnp.exp2(s_w[j] - m)
            denom = denom + jnp.sum(p, axis=-1, keepdims=True)
            o = o + jnp.dot(p.astype(BF16), vw[j][:, hs], preferred_element_type=F32)
        o_ref[:, hs] = (o / denom).astype(o_ref.dtype)


def _na_win_base(g, n_tiles):
    return 1 + jnp.clip(g - 2, 0, n_tiles - 1 - NA_WIN_TILES)


def _na_attn(larr, proj, slabs, slab_idx, n_heads, n_diff_heads):
    t = proj.shape[0]
    n_tiles = t // ROW_TILE
    w = n_heads * HEAD_DIM
    qcol = 3 * n_diff_heads * HEAD_DIM // w
    blk = (ROW_TILE, w)

    def win_spec(col, j):
        return pl.BlockSpec(blk, lambda g, l, ix: (_na_win_base(g, n_tiles) + j, col))

    in_specs = [pl.BlockSpec(blk, lambda g, l, ix: (g, qcol)),
                pl.BlockSpec(blk, lambda g, l, ix: (0, qcol + 1))]
    in_specs += [win_spec(qcol + 1, j) for j in range(NA_WIN_TILES)]
    in_specs += [pl.BlockSpec(blk, lambda g, l, ix: (0, qcol + 2))]
    in_specs += [win_spec(qcol + 2, j) for j in range(NA_WIN_TILES)]
    in_specs += [pl.BlockSpec((1,) + slabs.shape[1:], lambda g, l, ix: (l[0], 0, 0, 0, 0))]
    return pl.pallas_call(
        functools.partial(_na_kernel, n_heads=n_heads),
        grid_spec=pltpu.PrefetchScalarGridSpec(
            num_scalar_prefetch=2, grid=(n_tiles,),
            in_specs=in_specs,
            out_specs=pl.BlockSpec(blk, lambda g, l, ix: (g, 0))),
        out_shape=jax.ShapeDtypeStruct((t, w), BF16),
        compiler_params=_cparams(("arbitrary",)),
        name="na_attn",
    )(larr, slab_idx, *([proj] * 9), slabs)


def _na_bias_slabs(rpb, n_tiles):
    depth, n_h = rpb.shape[:2]
    rows = (n_tiles - 1) * NA_ROWS
    kr = min(WIN_R, rows)
    n_dr = 2 * WIN_R - 1
    n_kr = NA_WIN_TILES * NA_ROWS
    qc = np.arange(GRID_W)[:, None]
    kc = np.arange(GRID_W)[None, :]
    cs = np.clip(qc - WIN_C // 2, 0, GRID_W - WIN_C)
    col_valid = (kc >= cs) & (kc < cs + WIN_C)
    pad = GRID_W - WIN_C
    row = jnp.concatenate([jnp.repeat(rpb[..., :1], pad, axis=-1), rpb,
                           jnp.repeat(rpb[..., -1:], pad, axis=-1)], axis=-1)
    slabs = jnp.stack([row[..., GRID_W - 1 - c:2 * GRID_W - 1 - c] for c in range(GRID_W)], axis=-2)
    slabs = jnp.where(col_valid[None, None, None], slabs, NEG_INF) * LOG2E
    masked = jnp.full((depth, n_h, 1, GRID_W, GRID_W), NEG_INF * LOG2E, F32)
    slabs = jnp.concatenate([slabs, masked], axis=2)
    slab_idx = np.full((4, NA_ROWS, n_kr), n_dr, np.int32)
    for p, gl in enumerate((0, 1, n_tiles - 2)):
        qr = NA_ROWS * gl + np.arange(NA_ROWS)[:, None]
        win_start = NA_ROWS * int(np.clip(gl - 1, 0, n_tiles - 1 - NA_WIN_TILES))
        krow = win_start + np.arange(n_kr)[None, :]
        rs = np.clip(qr - kr // 2, 0, rows - kr)
        row_valid = (krow >= rs) & (krow < rs + kr)
        slab_idx[p + 1] = np.where(row_valid, krow - qr + (WIN_R - 1), n_dr)
    slabs = jnp.concatenate([slabs, slabs], axis=-1)
    return slabs, jnp.asarray(slab_idx.reshape(-1))


def _out_router_kernel(l_ref, d_ref, n_ref, wa_ref, wb_ref, x_ref, g1_ref, nw_ref, sh_ref, sc_ref,
                       wr_ref, xo_ref, h_ref, ids_ref, gates_ref):
    is_ctx = pl.program_id(0) == 0
    acc = jnp.dot(d_ref[...], wa_ref[0], preferred_element_type=F32)
    acc = acc + jnp.dot(n_ref[...], wb_ref[0], preferred_element_type=F32)
    x = x_ref[...] + _mod_row(g1_ref, is_ctx) * acc
    xo_ref[...] = x
    y = x * lax.rsqrt(jnp.mean(x * x, axis=-1, keepdims=True) + NORM_EPS) * nw_ref[0]
    y = y * (1.0 + _mod_row(sc_ref, is_ctx)) + _mod_row(sh_ref, is_ctx)
    h_ref[...] = y

    y_hi = y.astype(BF16)
    y_lo = (y - y_hi.astype(F32)).astype(BF16)
    lg = jnp.dot(y_hi, wr_ref[0, 0], preferred_element_type=F32)
    lg = lg + (jnp.dot(y_lo, wr_ref[0, 0], preferred_element_type=F32)
               + jnp.dot(y_hi, wr_ref[0, 1], preferred_element_type=F32))

    lane = lax.broadcasted_iota(jnp.int32, lg.shape, 1).astype(F32)
    big = jnp.float32(1e9)
    is_g = (lane >= N_EXPERTS) & (lane < N_EXPERTS + N_GROUPS)
    gl = jnp.where(is_g, lg, -jnp.inf)
    gmax = jnp.max(gl, axis=-1, keepdims=True)
    gsel = jnp.min(jnp.where(gl == gmax, lane, big), axis=-1, keepdims=True) - N_EXPERTS
    g_gate = 1.0 / jnp.sum(jnp.where(is_g, jnp.exp(gl - gmax), 0.0), axis=-1, keepdims=True)
    lo = gsel * EXPERTS_PER_GROUP
    el = jnp.where((lane >= lo) & (lane < lo + EXPERTS_PER_GROUP), lg, -jnp.inf)
    e1 = jnp.max(el, axis=-1, keepdims=True)
    i1 = jnp.min(jnp.where(el == e1, lane, big), axis=-1, keepdims=True)
    el2 = jnp.where(lane == i1, -jnp.inf, el)
    e2 = jnp.max(el2, axis=-1, keepdims=True)
    i2 = jnp.min(jnp.where(el2 == e2, lane, big), axis=-1, keepdims=True)
    r = jnp.exp(e2 - e1)
    w1 = 1.0 / (1.0 + r)
    w2 = r / (1.0 + r)
    ids_ref[...] = jnp.where(lane == 0, i1, jnp.where(lane == 1, i2, 0.0)).astype(jnp.int32)
    gates_ref[...] = jnp.where(lane == 0, g_gate * w1, jnp.where(lane == 1, g_gate * w2, 0.0))


def _out_router(larr, d_out, n_out, w_out_b, xa, mods, norm2_w, wr):
    t, d = xa.shape
    half = d_out.shape[1]
    depth = norm2_w.shape[0]
    tile = lambda width: pl.BlockSpec((ROW_TILE, width), lambda i, l: (i, 0))
    return pl.pallas_call(
        _out_router_kernel,
        grid_spec=pltpu.PrefetchScalarGridSpec(
            num_scalar_prefetch=1, grid=(t // ROW_TILE,),
            in_specs=[tile(half), tile(half),
                      pl.BlockSpec((1, half, d), lambda i, l: (l[0], 0, 0)),
                      pl.BlockSpec((1, half, d), lambda i, l: (l[0], 1, 0)),
                      tile(d), _mod_spec(d, 2),
                      pl.BlockSpec((1, 1, d), lambda i, l: (l[0], 0, 0)),
                      _mod_spec(d, 3), _mod_spec(d, 4),
                      pl.BlockSpec((1, 2, d, HEAD_DIM), lambda i, l: (l[0], 0, 0, 0))],
            out_specs=[tile(d), tile(d), tile(HEAD_DIM), tile(HEAD_DIM)]),
        out_shape=[jax.ShapeDtypeStruct((t, d), F32), jax.ShapeDtypeStruct((t, d), F32),
                   jax.ShapeDtypeStruct((t, HEAD_DIM), jnp.int32),
                   jax.ShapeDtypeStruct((t, HEAD_DIM), F32)],
        compiler_params=_cparams(("arbitrary",)),
        name="out_router",
    )(larr, d_out, n_out, w_out_b, w_out_b, xa, mods, norm2_w.reshape(depth, 1, d), mods, mods, wr)


def _expert_kernel(l_ref, be_ref, st_ref, nu_ref, h_hbm, w1_ref, w3_ref, w2_ref, y_ref,
                   xbuf0, xbuf1, wb1, wb3, wb2, sem):
    b = pl.program_id(0)
    n_blocks = pl.num_programs(0)
    xbufs = (xbuf0, xbuf1)
    n_chunks = MOE_BLOCK // GATHER_CHUNK

    def rows_of(blk):
        return jnp.where(blk < n_blocks, nu_ref[jnp.minimum(blk, n_blocks - 1)], 0)

    def gather(blk, slot):
        n_rows = rows_of(blk)
        for c in range(n_chunks):
            @pl.when(c * GATHER_CHUNK < n_rows)
            def _():
                for r in range(c * GATHER_CHUNK, (c + 1) * GATHER_CHUNK):
                    tok = st_ref[blk * MOE_BLOCK + r]
                    pltpu.make_async_copy(h_hbm.at[pl.ds(tok, 1), :],
                                          xbufs[slot].at[pl.ds(r, 1), :], sem.at[slot]).start()

    def wait_rows(blk, slot):
        n_rows = rows_of(blk)
        for c in range(n_chunks):
            @pl.when(c * GATHER_CHUNK < n_rows)
            def _():
                pltpu.make_async_copy(h_hbm.at[pl.ds(0, GATHER_CHUNK), :],
                                      xbufs[slot].at[pl.ds(c * GATHER_CHUNK, GATHER_CHUNK), :],
                                      sem.at[slot]).wait()

    @pl.when(b == 0)
    def _():
        xbuf0[...] = jnp.zeros(xbuf0.shape, F32)
        xbuf1[...] = jnp.zeros(xbuf1.shape, F32)
        gather(0, 0)

    used = rows_of(b) > 0

    @pl.when(used & ((b == 0) | (be_ref[b] != be_ref[jnp.maximum(b - 1, 0)])))
    def _():
        wb1[...] = w1_ref[0, 0].astype(BF16)
        wb3[...] = w3_ref[0, 0].astype(BF16)
        wb2[...] = w2_ref[0, 0].astype(BF16)

    def run(slot):
        wait_rows(b, slot)
        gather(b + 1, 1 - slot)
        xb = xbufs[slot][...].astype(BF16)
        a = jnp.dot(xb, wb1[...], preferred_element_type=F32)
        g = jnp.dot(xb, wb3[...], preferred_element_type=F32)
        mid = (a / (1.0 + jnp.exp(-a))) * g
        y_ref[...] = jnp.dot(mid.astype(BF16), wb2[...], preferred_element_type=F32)

    for slot in range(2):
        pl.when(used & (b % 2 == slot))(functools.partial(run, slot))

    @pl.when(jnp.logical_not(used))
    def _():
        y_ref[...] = jnp.zeros(y_ref.shape, y_ref.dtype)


def _experts(larr, block_expert, slot_tok, block_rows, h2, w1, w3, w2):
    t, d = h2.shape
    ff = w1.shape[-1]
    n_blocks = block_expert.shape[0]
    return pl.pallas_call(
        _expert_kernel,
        grid_spec=pltpu.PrefetchScalarGridSpec(
            num_scalar_prefetch=4, grid=(n_blocks,),
            in_specs=[pl.BlockSpec(memory_space=pl.ANY),
                      pl.BlockSpec((1, 1, d, ff), lambda b, l, be, st, nu: (l[0], be[b], 0, 0)),
                      pl.BlockSpec((1, 1, d, ff), lambda b, l, be, st, nu: (l[0], be[b], 0, 0)),
                      pl.BlockSpec((1, 1, ff, d), lambda b, l, be, st, nu: (l[0], be[b], 0, 0))],
            out_specs=pl.BlockSpec((MOE_BLOCK, d), lambda b, l, be, st, nu: (b, 0)),
            scratch_shapes=[pltpu.VMEM((MOE_BLOCK, d), F32), pltpu.VMEM((MOE_BLOCK, d), F32),
                            pltpu.VMEM((d, ff), BF16), pltpu.VMEM((d, ff), BF16),
                            pltpu.VMEM((ff, d), BF16),
                            pltpu.SemaphoreType.DMA((2,))]),
        out_shape=jax.ShapeDtypeStruct((n_blocks * MOE_BLOCK, d), F32),
        compiler_params=_cparams(("arbitrary",)),
        name="experts",
    )(larr, block_expert, slot_tok, block_rows, h2, w1, w3, w2)


def _combine_kernel(l_ref, dest_ref, y_hbm, x_ref, gates_ref, g2_ref, nw_ref, sh_ref, sc_ref,
                    o_ref, h_ref, ybuf, sem):
    i = pl.program_id(0)
    is_ctx = i == 0
    buf = i % 2

    def gather(tile, dst):
        def issue(r, carry):
            for k in range(2):
                slot = dest_ref[(tile * ROW_TILE + r) * 2 + k]
                pltpu.make_async_copy(y_hbm.at[pl.ds(slot, 1), :],
                                      ybuf.at[dst, pl.ds(k * ROW_TILE + r, 1), :],
                                      sem.at[dst]).start()
            return carry
        lax.fori_loop(0, ROW_TILE, issue, 0)

    @pl.when(i == 0)
    def _():
        gather(0, 0)

    pltpu.make_async_copy(y_hbm.at[pl.ds(0, 2 * ROW_TILE), :], ybuf.at[buf], sem.at[buf]).wait()

    @pl.when(i + 1 < pl.num_programs(0))
    def _():
        gather(i + 1, 1 - buf)

    gates = gates_ref[...]
    y = (gates[:, 0:1] * ybuf[buf, 0:ROW_TILE, :]
         + gates[:, 1:2] * ybuf[buf, ROW_TILE:2 * ROW_TILE, :])
    x = x_ref[...] + _mod_row(g2_ref, is_ctx) * y
    o_ref[...] = x
    h = x * lax.rsqrt(jnp.mean(x * x, axis=-1, keepdims=True) + NORM_EPS) * nw_ref[0]
    h = h * (1.0 + _mod_row(sc_ref, is_ctx)) + _mod_row(sh_ref, is_ctx)
    h_ref[...] = h.astype(h_ref.dtype)


def _combine(larr, dest, y_slots, x_new, gates, mods, norm1_w):
    t, d = x_new.shape
    depth = norm1_w.shape[0]
    nxt = lambda l: jnp.minimum(l[0] + 1, depth - 1)
    tile = pl.BlockSpec((ROW_TILE, d), lambda i, l, dd: (i, 0))
    return pl.pallas_call(
        _combine_kernel,
        grid_spec=pltpu.PrefetchScalarGridSpec(
            num_scalar_prefetch=2, grid=(t // ROW_TILE,),
            in_specs=[pl.BlockSpec(memory_space=pl.ANY),
                      tile,
                      pl.BlockSpec((ROW_TILE, HEAD_DIM), lambda i, l, dd: (i, 0)),
                      pl.BlockSpec((1, 8, d), lambda i, l, dd: (l[0], 0, 5)),
                      pl.BlockSpec((1, 1, d), lambda i, l, dd: (nxt(l), 0, 0)),
                      pl.BlockSpec((1, 8, d), lambda i, l, dd: (nxt(l), 0, 0)),
                      pl.BlockSpec((1, 8, d), lambda i, l, dd: (nxt(l), 0, 1))],
            out_specs=[tile, tile],
            scratch_shapes=[pltpu.VMEM((2, 2 * ROW_TILE, d), F32),
                            pltpu.SemaphoreType.DMA((2,))]),
        out_shape=[jax.ShapeDtypeStruct((t, d), F32), jax.ShapeDtypeStruct((t, d), BF16)],
        compiler_params=_cparams(("arbitrary",)),
        name="combine",
    )(larr, dest, y_slots, x_new, gates, mods, norm1_w.reshape(depth, 1, d), mods, mods)


def _dispatch(ids):
    t = ids.shape[0]
    n_assign = 2 * t
    e_flat = ids.reshape(n_assign)
    onehot = (e_flat[:, None] == jnp.arange(N_EXPERTS, dtype=jnp.int32)[None, :]).astype(jnp.int32)
    csum = jnp.cumsum(onehot, axis=0)
    rank = jnp.sum(onehot * (csum - 1), axis=1)
    counts = csum[-1]
    padded = (counts + MOE_BLOCK - 1) // MOE_BLOCK * MOE_BLOCK
    padded_end = jnp.cumsum(padded)
    padded_start = padded_end - padded
    dest = (padded_start[e_flat] + rank).astype(jnp.int32)
    n_blocks = -(-n_assign // MOE_BLOCK) + N_EXPERTS
    tok = jnp.arange(n_assign, dtype=jnp.int32) // 2
    slot_tok = jnp.zeros((n_blocks * MOE_BLOCK,), jnp.int32).at[dest].set(tok)
    block_start = jnp.arange(n_blocks, dtype=jnp.int32) * MOE_BLOCK
    block_expert = jnp.clip(jnp.searchsorted(padded_end, block_start, side='right'),
                            0, N_EXPERTS - 1).astype(jnp.int32)
    filled_end = (padded_start + counts)[block_expert]
    block_rows = jnp.clip(filled_end - block_start, 0, MOE_BLOCK).astype(jnp.int32)
    return dest, slot_tok, block_expert, block_rows


def _diff_lane_fields():
    lane = np.arange(HEAD_DIM)
    part, m, half, f = lane // 64, (lane % 64) // 32, (lane % 32) // 16, lane % 16
    return part, m, half, f


def _rope_tables(ctx_len, seq):
    part, m, half, f = _diff_lane_fields()
    nf = HEAD_DIM // 8
    inv = ROPE_BASE ** (-jnp.arange(nf, dtype=F32) / nf)
    pos = jnp.arange(seq)
    prow = (pos // GRID_W).astype(F32)
    pcol = (pos % GRID_W).astype(F32)
    p = jnp.where(jnp.asarray(half)[None, :] == 0, prow[:, None], pcol[:, None])
    ang = p * inv[jnp.asarray(f)][None, :]
    sign = jnp.where(jnp.asarray(part) == 0, -1.0, 1.0).astype(F32)[None, :]
    cos = jnp.concatenate([jnp.ones((ctx_len, HEAD_DIM), F32), jnp.cos(ang)], axis=0)
    sin = jnp.concatenate([jnp.zeros((ctx_len, HEAD_DIM), F32), jnp.sin(ang) * sign], axis=0)
    return cos, sin


def _permute_diff_cols(w, n_heads):
    lead = w.shape[:-1]
    w = w.reshape(lead + (n_heads, 2, 2, 2, 16))
    nd = len(lead)
    w = jnp.transpose(w, tuple(range(nd)) + (nd, nd + 3, nd + 1, nd + 2, nd + 4))
    return w.reshape(lead + (n_heads * HEAD_DIM,))


def _qk_weight_table(diff_q_norm, diff_k_norm, na_q_norm, na_k_norm):
    part, m, half, f = _diff_lane_fields()
    src = half * 32 + part * 16 + f
    dqk = HEAD_DIM // 2
    dq = diff_q_norm[:, src] * (dqk ** -0.5 * LOG2E)
    dk = diff_k_norm[:, src]
    nq = na_q_norm * (HEAD_DIM ** -0.5 * LOG2E)
    rows = jnp.stack([dq, dk, nq, na_k_norm], axis=1)
    return jnp.concatenate([rows, jnp.zeros_like(rows)], axis=1)


def kernel(x, c, ctx, c_ctx, ada_w, ada_b, norm1_w, norm2_w, w_in, w_out, diff_q_norm, diff_k_norm,
           diff_lq1, diff_lk1, diff_lq2, diff_lk2, diff_subln, na_q_norm, na_k_norm, na_rpb,
           moe_w_group, moe_w_expert, moe_w1, moe_w3, moe_w2):
    _, seq, d = x.shape
    ctx_len = ctx.shape[1]
    depth = ada_w.shape[0]
    assert ctx_len == ROW_TILE and seq % ROW_TILE == 0 and x.shape[0] == 1
    n_heads = d // HEAD_DIM
    n_diff = n_heads // 2
    n_na = n_heads - n_diff
    seg = n_diff * HEAD_DIM
    t = ctx_len + seq
    n_tiles = t // ROW_TILE

    xa = jnp.concatenate([ctx[0], x[0]], axis=0)
    c2 = jnp.zeros((8, d), F32).at[0].set(c[0]).at[1].set(c_ctx)
    mods = _adaln(c2, ada_w, ada_b)

    w_in_b = w_in.astype(BF16)
    w_q = _permute_diff_cols(w_in_b[..., :seg], n_diff)
    w_k = _permute_diff_cols(w_in_b[..., seg:2 * seg], n_diff)
    w_out_b = w_out.astype(BF16)
    qkw = _qk_weight_table(diff_q_norm, diff_k_norm, na_q_norm, na_k_norm)
    cos_t, sin_t = _rope_tables(ctx_len, seq)
    na_slabs, na_slab_idx = _na_bias_slabs(na_rpb, n_tiles)

    lam_init = jnp.asarray([0.8 - 0.6 * math.exp(-0.3 * l) for l in range(depth)], F32)
    lam = (jnp.exp(jnp.sum(diff_lq1 * diff_lk1, axis=-1)) - jnp.exp(jnp.sum(diff_lq2 * diff_lk2, axis=-1))
           + lam_init)
    bound = (HEAD_DIM // 2) * jnp.max(jnp.abs(qkw[:, 0]), axis=-1) * jnp.max(jnp.abs(qkw[:, 1]), axis=-1)
    bound = bound * 1.02 + 0.01
    bound = jnp.where(bound <= MAX_STATIC_SHIFT, bound, -1.0)
    lam_tab = jnp.stack([lam, 1.0 - lam_init, bound], axis=1).reshape(DIFF_TAB * depth)

    wr = jnp.concatenate([moe_w_expert, moe_w_group,
                          jnp.zeros((depth, d, HEAD_DIM - N_EXPERTS - N_GROUPS), F32)], axis=-1)
    wr_hi = wr.astype(BF16)
    wr_lo = (wr - wr_hi.astype(F32)).astype(BF16)
    wr_split = jnp.stack([wr_hi, wr_lo], axis=1)

    h = _norm_mod(jnp.zeros((1,), jnp.int32), xa, norm1_w, mods)
    for l in range(depth):
        larr = jnp.full((1,), l, jnp.int32)
        proj = _in_proj(larr, h, w_q, w_k, w_in_b, qkw, cos_t, sin_t)
        d_out = _diff_attn(larr, lam_tab, proj, diff_subln, n_diff, ctx_len)
        n_out = _na_attn(larr, proj, na_slabs, na_slab_idx, n_na, n_diff)
        x_new, h2, ids, gates = _out_router(larr, d_out, n_out, w_out_b, xa, mods, norm2_w, wr_split)
        dest, slot_tok, block_expert, block_rows = _dispatch(ids[:, :2])
        y_slots = _experts(larr, block_expert, slot_tok, block_rows, h2, moe_w1, moe_w3, moe_w2)
        xa, h = _combine(larr, dest, y_slots, x_new, gates, mods, norm1_w)
    return xa[ctx_len:][None]
```

```python
import functools
import math

import numpy as np
import jax
import jax.numpy as jnp
from jax import lax
from jax.experimental import pallas as pl
from jax.experimental.pallas import tpu as pltpu

F32 = jnp.float32
BF16 = jnp.bfloat16

GRID_W = 64
HEAD_DIM = 128
WIN_R = 8
WIN_C = 16
ROPE_BASE = 10000.0
N_GROUPS = 4
EXPERTS_PER_GROUP = 8
N_EXPERTS = N_GROUPS * EXPERTS_PER_GROUP
NORM_EPS = 1e-6
NEG_INF = -1e30
LOG2E = 1.4426950408889634

ROW_TILE = 256
NA_ROWS = ROW_TILE // GRID_W
NA_WIN_TILES = 3
MOE_BLOCK = 256
GATHER_CHUNK = 32
ADA_TN = 1536
IN_PROJ_TM = (384, 256)
DIFF_TK = 512
DIFF_TK_ALL = (1408, 768, 640, 512, 256)
VMEM_LIMIT = 56 * 1024 * 1024


def _cparams(sem, vmem=VMEM_LIMIT):
    return pltpu.CompilerParams(dimension_semantics=sem, vmem_limit_bytes=vmem)


def _nt_dot(a, b):
    return lax.dot_general(a, b, (((1,), (1,)), ((), ())), preferred_element_type=F32)


def _adaln_kernel(c_ref, w_ref, b_ref, o_ref):
    c = c_ref[...]
    a = c / (1.0 + jnp.exp(-c))
    o_ref[0] = jnp.dot(a.astype(BF16), w_ref[0].astype(BF16),
                       preferred_element_type=F32) + b_ref[0]


def _adaln(c2, ada_w, ada_b):
    depth, d, six_d = ada_w.shape
    tn = ADA_TN
    return pl.pallas_call(
        _adaln_kernel,
        grid=(depth, six_d // tn),
        in_specs=[pl.BlockSpec((8, d), lambda l, j: (0, 0)),
                  pl.BlockSpec((1, d, tn), lambda l, j: (l, 0, j)),
                  pl.BlockSpec((1, 1, tn), lambda l, j: (l, 0, j))],
        out_specs=pl.BlockSpec((1, 8, tn), lambda l, j: (l, 0, j)),
        out_shape=jax.ShapeDtypeStruct((depth, 8, six_d), F32),
        compiler_params=_cparams(("arbitrary", "arbitrary")),
        name="adaln",
    )(c2, ada_w, ada_b.reshape(depth, 1, six_d))


def _mod_spec(d, chunk):
    return pl.BlockSpec((1, 8, d), lambda i, l: (l[0], 0, chunk))


def _mod_row(ref, is_ctx):
    return jnp.where(is_ctx, ref[0, 1:2, :], ref[0, 0:1, :])


def _norm_mod_kernel(l_ref, x_ref, nw_ref, sh_ref, sc_ref, o_ref):
    is_ctx = pl.program_id(0) == 0
    x = x_ref[...]
    y = x * lax.rsqrt(jnp.mean(x * x, axis=-1, keepdims=True) + NORM_EPS) * nw_ref[0]
    y = y * (1.0 + _mod_row(sc_ref, is_ctx)) + _mod_row(sh_ref, is_ctx)
    o_ref[...] = y.astype(o_ref.dtype)


def _norm_mod(larr, xa, norm_w, mods):
    t, d = xa.shape
    depth = norm_w.shape[0]
    return pl.pallas_call(
        _norm_mod_kernel,
        grid_spec=pltpu.PrefetchScalarGridSpec(
            num_scalar_prefetch=1, grid=(t // ROW_TILE,),
            in_specs=[pl.BlockSpec((ROW_TILE, d), lambda i, l: (i, 0)),
                      pl.BlockSpec((1, 1, d), lambda i, l: (l[0], 0, 0)),
                      _mod_spec(d, 0), _mod_spec(d, 1)],
            out_specs=pl.BlockSpec((ROW_TILE, d), lambda i, l: (i, 0))),
        out_shape=jax.ShapeDtypeStruct((t, d), BF16),
        compiler_params=_cparams(("arbitrary",)),
        name="norm_mod",
    )(larr, xa, norm_w.reshape(depth, 1, d), mods, mods)


def _in_proj_kernel(l_ref, x_ref, wq_ref, wk_ref, wm_ref, wh_ref, qkw_ref, cos_ref, sin_ref, o_ref,
                    acc_ref, *, seg):
    n_chunks = seg // HEAD_DIM
    x = x_ref[...]
    a = lax.broadcasted_iota(jnp.int32, (HEAD_DIM, HEAD_DIM), 0)
    b = lax.broadcasted_iota(jnp.int32, (HEAD_DIM, HEAD_DIM), 1)
    map_mat = ((a & 32) == (b & 32)).astype(BF16)
    head_mat = jnp.ones((HEAD_DIM, HEAD_DIM), BF16)
    weights = (wq_ref[0], wk_ref[0], wm_ref[0, :, 0:seg], wm_ref[0, :, seg:2 * seg],
               wh_ref[0, :, 0:seg], wh_ref[0, :, seg:2 * seg])
    norm_row = {0: 0, 1: 1, 3: 2, 4: 3}

    def normed(buf, c, group_mat, inv_n, wrow):
        xc = acc_ref[buf, :, c * HEAD_DIM:(c + 1) * HEAD_DIM]
        ss = jnp.dot((xc * xc).astype(BF16), group_mat, preferred_element_type=F32)
        return xc * lax.rsqrt(ss * inv_n + NORM_EPS) * wrow

    for j in range(6):
        buf = j % 2
        acc_ref[buf] = jnp.dot(x, weights[j], preferred_element_type=F32)
        for c in range(n_chunks):
            cols = slice(j * seg + c * HEAD_DIM, j * seg + (c + 1) * HEAD_DIM)
            if j in (2, 5):
                y = acc_ref[buf, :, c * HEAD_DIM:(c + 1) * HEAD_DIM]
            elif j < 2:
                wrow = qkw_ref[0, norm_row[j]:norm_row[j] + 1, :]
                y = normed(buf, c, map_mat, 2.0 / HEAD_DIM, wrow)
                y = y * cos_ref[...] + pltpu.roll(y, HEAD_DIM // 2, 1) * sin_ref[...]
            else:
                wrow = qkw_ref[0, norm_row[j]:norm_row[j] + 1, :]
                y = normed(buf, c, head_mat, 1.0 / HEAD_DIM, wrow)
            o_ref[:, cols] = y.astype(o_ref.dtype)


def _pick_tile(n, candidates):
    for c in candidates:
        if n % c == 0:
            return c
    raise ValueError(f"no tile for {n}")


def _in_proj(larr, h, w_q, w_k, w_in_b, qkw, cos_t, sin_t):
    t, d = h.shape
    seg = w_q.shape[2]
    tm = _pick_tile(t, IN_PROJ_TM)
    once = pl.Buffered(1)

    def weight_spec(width, col):
        return pl.BlockSpec((1, d, width), lambda i, l: (l[0], 0, col), pipeline_mode=once)

    return pl.pallas_call(
        functools.partial(_in_proj_kernel, seg=seg),
        grid_spec=pltpu.PrefetchScalarGridSpec(
            num_scalar_prefetch=1, grid=(t // tm,),
            in_specs=[pl.BlockSpec((tm, d), lambda i, l: (i, 0)),
                      weight_spec(seg, 0), weight_spec(seg, 0),
                      weight_spec(2 * seg, 1), weight_spec(2 * seg, 2),
                      pl.BlockSpec((1, 8, HEAD_DIM), lambda i, l: (l[0], 0, 0)),
                      pl.BlockSpec((tm, HEAD_DIM), lambda i, l: (i, 0)),
                      pl.BlockSpec((tm, HEAD_DIM), lambda i, l: (i, 0))],
            out_specs=pl.BlockSpec((tm, 6 * seg), lambda i, l: (i, 0)),
            scratch_shapes=[pltpu.VMEM((2, tm, seg), F32)]),
        out_shape=jax.ShapeDtypeStruct((t, 6 * seg), BF16),
        compiler_params=_cparams(("arbitrary",)),
        name="in_proj",
    )(larr, h, w_q, w_k, w_in_b, w_in_b, qkw, cos_t, sin_t)


DIFF_HEADS_PER_STEP = (4, 2, 1)
DIFF_TAB = 3
MAX_STATIC_SHIFT = 60.0


def _diff_attn_kernel(l_ref, lam_ref, q_ref, k_ref, v_ref, sw_ref, o_ref, m_ref, s_ref, part_ref,
                      acc_ref, *, ctx_len, tk, tk_all, heads):
    i = pl.program_id(1)
    layer = l_ref[0]
    tq = q_ref.shape[0]
    n_lat = (k_ref.shape[0] - ctx_len) // tk
    lam = lam_ref[DIFF_TAB * layer]
    out_scale = lam_ref[DIFF_TAB * layer + 1]
    bound = lam_ref[DIFF_TAB * layer + 2]
    lane = lax.broadcasted_iota(jnp.int32, (1, HEAD_DIM), 1)
    in_map0 = (lane & 32) == 0

    def head_cols(hh):
        return slice(hh * HEAD_DIM, (hh + 1) * HEAD_DIM)

    def masked_q(hh):
        q = q_ref[:, head_cols(hh)]
        return (jnp.where(in_map0, q, jnp.zeros_like(q)), jnp.where(in_map0, jnp.zeros_like(q), q))

    def finish(hh, l0, l1):
        o = acc_ref[hh, 0] / l0 - lam * (acc_ref[hh, 1] / l1)
        o = o * lax.rsqrt(jnp.mean(o * o, axis=-1, keepdims=True) + NORM_EPS) * sw_ref[0] * out_scale
        o_ref[:, head_cols(hh)] = o.astype(o_ref.dtype)

    @pl.when(bound >= 0.0)
    def _():
        def attend(hh, n_keys, chunk):
            hs = head_cols(hh)
            q_both = jnp.concatenate(masked_q(hh), axis=0)
            for lo in range(0, n_keys, chunk):
                p = jnp.exp2(_nt_dot(q_both, k_ref[lo:lo + chunk, hs]) - bound)
                prt = p[:, 0:HEAD_DIM]
                for j in range(1, chunk // HEAD_DIM):
                    prt = prt + p[:, j * HEAD_DIM:(j + 1) * HEAD_DIM]
                pv = jnp.dot(p.astype(BF16), v_ref[lo:lo + chunk, hs], preferred_element_type=F32)
                for mi in range(2):
                    if lo == 0:
                        part_ref[hh, mi] = prt[mi * tq:(mi + 1) * tq]
                        acc_ref[hh, mi] = pv[mi * tq:(mi + 1) * tq]
                    else:
                        part_ref[hh, mi] += prt[mi * tq:(mi + 1) * tq]
                        acc_ref[hh, mi] += pv[mi * tq:(mi + 1) * tq]

        def run(n_keys, chunk):
            for hh in range(heads):
                attend(hh, n_keys, chunk)
                finish(hh, jnp.sum(part_ref[hh, 0], axis=-1, keepdims=True),
                       jnp.sum(part_ref[hh, 1], axis=-1, keepdims=True))

        @pl.when(i == 0)
        def _():
            run(ctx_len, ctx_len)

        @pl.when(i > 0)
        def _():
            run(k_ref.shape[0], tk_all)

    @pl.when(bound < 0.0)
    def _():
        for hh in range(heads):
            hs = head_cols(hh)
            qs = masked_q(hh)
            m_ref[...] = jnp.full(m_ref.shape, -jnp.inf, F32)
            s_ref[...] = jnp.zeros(s_ref.shape, F32)
            acc_ref[hh] = jnp.zeros(acc_ref.shape[1:], F32)

            def step(kc, vc, hh=hh, qs=qs):
                for mi in range(2):
                    s = _nt_dot(qs[mi], kc)
                    m_prev = m_ref[mi]
                    m_new = jnp.maximum(m_prev, jnp.max(s, axis=-1, keepdims=True))
                    alpha = jnp.exp2(m_prev - m_new)
                    p = jnp.exp2(s - m_new)
                    s_ref[mi] = alpha * s_ref[mi] + jnp.sum(p, axis=-1, keepdims=True)
                    acc_ref[hh, mi] = alpha * acc_ref[hh, mi] + jnp.dot(
                        p.astype(BF16), vc, preferred_element_type=F32)
                    m_ref[mi] = m_new

            step(k_ref[0:ctx_len, hs], v_ref[0:ctx_len, hs])

            def body(c, carry, step=step, hs=hs):
                off = pl.multiple_of(ctx_len + c * tk, math.gcd(ctx_len, tk))
                step(k_ref[pl.ds(off, tk), hs], v_ref[pl.ds(off, tk), hs])
                return carry
            lax.fori_loop(0, jnp.where(i == 0, 0, n_lat), body, 0)
            finish(hh, s_ref[0], s_ref[1])


def _diff_attn(larr, lam_tab, proj, subln, n_heads, ctx_len):
    t = proj.shape[0]
    depth = subln.shape[0]
    tq = ROW_TILE
    tk = _pick_tile(t - ctx_len, (DIFF_TK, 256))
    heads = _pick_tile(n_heads, DIFF_HEADS_PER_STEP)
    n_hb = n_heads // heads
    wb = heads * HEAD_DIM
    return pl.pallas_call(
        functools.partial(_diff_attn_kernel, ctx_len=ctx_len, tk=tk,
                          tk_all=_pick_tile(t, DIFF_TK_ALL), heads=heads),
        grid_spec=pltpu.PrefetchScalarGridSpec(
            num_scalar_prefetch=1, grid=(n_hb, t // tq),
            in_specs=[pl.BlockSpec(memory_space=pltpu.SMEM),
                      pl.BlockSpec((tq, wb), lambda h, i, l: (i, h)),
                      pl.BlockSpec((t, wb), lambda h, i, l: (0, n_hb + h)),
                      pl.BlockSpec((t, wb), lambda h, i, l: (0, 2 * n_hb + h)),
                      pl.BlockSpec((1, 1, HEAD_DIM), lambda h, i, l: (l[0], 0, 0))],
            out_specs=pl.BlockSpec((tq, wb), lambda h, i, l: (i, h)),
            scratch_shapes=[pltpu.VMEM((2, tq, 1), F32), pltpu.VMEM((2, tq, 1), F32),
                            pltpu.VMEM((heads, 2, tq, HEAD_DIM), F32),
                            pltpu.VMEM((heads, 2, tq, HEAD_DIM), F32)]),
        out_shape=jax.ShapeDtypeStruct((t, n_heads * HEAD_DIM), BF16),
        compiler_params=_cparams(("arbitrary", "arbitrary")),
        name="diff_attn",
    )(larr, lam_tab, proj, proj, proj, subln.reshape(depth, 1, HEAD_DIM))


def _na_kernel(l_ref, idx_ref, q_ref, kc_ref, k0_ref, k1_ref, k2_ref, vc_ref, v0_ref, v1_ref, v2_ref,
               slab_ref, o_ref, *, n_heads):
    kw = (k0_ref, k1_ref, k2_ref)
    vw = (v0_ref, v1_ref, v2_ref)
    g = pl.program_id(0)
    n_tiles = pl.num_programs(0)
    pat = jnp.where(g == 0, 0, jnp.where(g == 1, 1, jnp.where(g == n_tiles - 1, 3, 2)))
    n_kr = NA_WIN_TILES * NA_ROWS
    low_lanes = lax.broadcasted_iota(jnp.int32, (GRID_W, 2 * GRID_W), 1) < GRID_W

    def bias_tile(h, j):
        rows = []
        for a in range(NA_ROWS):
            pairs = []
            for b in range(NA_ROWS * j, NA_ROWS * (j + 1), 2):
                i0 = idx_ref[(pat * NA_ROWS + a) * n_kr + b]
                i1 = idx_ref[(pat * NA_ROWS + a) * n_kr + b + 1]
                pairs.append(jnp.where(low_lanes, slab_ref[0, h, i0], slab_ref[0, h, i1]))
            rows.append(jnp.concatenate(pairs, axis=1))
        return jnp.concatenate(rows, axis=0)

    for h in range(n_heads):
        hs = slice(h * HEAD_DIM, (h + 1) * HEAD_DIM)
        qh = q_ref[:, hs]
        s_c = _nt_dot(qh, kc_ref[:, hs])
        s_w = [_nt_dot(qh, kw[j][:, hs]) + bias_tile(h, j) for j in range(NA_WIN_TILES)]
        m = jnp.max(s_c, axis=-1, keepdims=True)
        for s in s_w:
            m = jnp.maximum(m, jnp.max(s, axis=-1, keepdims=True))
        p_c = jnp.exp2(s_c - m)
        denom = jnp.sum(p_c, axis=-1, keepdims=True)
        o = jnp.dot(p_c.astype(BF16), vc_ref[:, hs], preferred_element_type=F32)
        for j in range(NA_WIN_TILES):
            p = jnp.exp2(s_w[j] - m)
            denom = denom + jnp.sum(p, axis=-1, keepdims=True)
            o = o + jnp.dot(p.astype(BF16), vw[j][:, hs], preferred_element_type=F32)
        o_ref[:, hs] = (o / denom).astype(o_ref.dtype)


def _na_win_base(g, n_tiles):
    return 1 + jnp.clip(g - 2, 0, n_tiles - 1 - NA_WIN_TILES)


def _na_attn(larr, proj, slabs, slab_idx, n_heads, n_diff_heads):
    t = proj.shape[0]
    n_tiles = t // ROW_TILE
    w = n_heads * HEAD_DIM
    qcol = 3 * n_diff_heads * HEAD_DIM // w
    blk = (ROW_TILE, w)

    def win_spec(col, j):
        return pl.BlockSpec(blk, lambda g, l, ix: (_na_win_base(g, n_tiles) + j, col))

    in_specs = [pl.BlockSpec(blk, lambda g, l, ix: (g, qcol)),
                pl.BlockSpec(blk, lambda g, l, ix: (0, qcol + 1))]
    in_specs += [win_spec(qcol + 1, j) for j in range(NA_WIN_TILES)]
    in_specs += [pl.BlockSpec(blk, lambda g, l, ix: (0, qcol + 2))]
    in_specs += [win_spec(qcol + 2, j) for j in range(NA_WIN_TILES)]
    in_specs += [pl.BlockSpec((1,) + slabs.shape[1:], lambda g, l, ix: (l[0], 0, 0, 0, 0))]
    return pl.pallas_call(
        functools.partial(_na_kernel, n_heads=n_heads),
        grid_spec=pltpu.PrefetchScalarGridSpec(
            num_scalar_prefetch=2, grid=(n_tiles,),
            in_specs=in_specs,
            out_specs=pl.BlockSpec(blk, lambda g, l, ix: (g, 0))),
        out_shape=jax.ShapeDtypeStruct((t, w), BF16),
        compiler_params=_cparams(("arbitrary",)),
        name="na_attn",
    )(larr, slab_idx, *([proj] * 9), slabs)


def _na_bias_slabs(rpb, n_tiles):
    depth, n_h = rpb.shape[:2]
    rows = (n_tiles - 1) * NA_ROWS
    kr = min(WIN_R, rows)
    n_dr = 2 * WIN_R - 1
    n_kr = NA_WIN_TILES * NA_ROWS
    qc = np.arange(GRID_W)[:, None]
    kc = np.arange(GRID_W)[None, :]
    cs = np.clip(qc - WIN_C // 2, 0, GRID_W - WIN_C)
    col_valid = (kc >= cs) & (kc < cs + WIN_C)
    pad = GRID_W - WIN_C
    row = jnp.concatenate([jnp.repeat(rpb[..., :1], pad, axis=-1), rpb,
                           jnp.repeat(rpb[..., -1:], pad, axis=-1)], axis=-1)
    slabs = jnp.stack([row[..., GRID_W - 1 - c:2 * GRID_W - 1 - c] for c in range(GRID_W)], axis=-2)
    slabs = jnp.where(col_valid[None, None, None], slabs, NEG_INF) * LOG2E
    masked = jnp.full((depth, n_h, 1, GRID_W, GRID_W), NEG_INF * LOG2E, F32)
    slabs = jnp.concatenate([slabs, masked], axis=2)
    slab_idx = np.full((4, NA_ROWS, n_kr), n_dr, np.int32)
    for p, gl in enumerate((0, 1, n_tiles - 2)):
        qr = NA_ROWS * gl + np.arange(NA_ROWS)[:, None]
        win_start = NA_ROWS * int(np.clip(gl - 1, 0, n_tiles - 1 - NA_WIN_TILES))
        krow = win_start + np.arange(n_kr)[None, :]
        rs = np.clip(qr - kr // 2, 0, rows - kr)
        row_valid = (krow >= rs) & (krow < rs + kr)
        slab_idx[p + 1] = np.where(row_valid, krow - qr + (WIN_R - 1), n_dr)
    slabs = jnp.concatenate([slabs, slabs], axis=-1)
    return slabs, jnp.asarray(slab_idx.reshape(-1))


def _out_router_kernel(l_ref, d_ref, n_ref, wa_ref, wb_ref, x_ref, g1_ref, nw_ref, sh_ref, sc_ref,
                       wr_ref, xo_ref, h_ref, ids_ref, gates_ref):
    is_ctx = pl.program_id(0) == 0
    acc = jnp.dot(d_ref[...], wa_ref[0], preferred_element_type=F32)
    acc = acc + jnp.dot(n_ref[...], wb_ref[0], preferred_element_type=F32)
    x = x_ref[...] + _mod_row(g1_ref, is_ctx) * acc
    xo_ref[...] = x
    y = x * lax.rsqrt(jnp.mean(x * x, axis=-1, keepdims=True) + NORM_EPS) * nw_ref[0]
    y = y * (1.0 + _mod_row(sc_ref, is_ctx)) + _mod_row(sh_ref, is_ctx)
    h_ref[...] = y

    y_hi = y.astype(BF16)
    y_lo = (y - y_hi.astype(F32)).astype(BF16)
    lg = jnp.dot(y_hi, wr_ref[0, 0], preferred_element_type=F32)
    lg = lg + (jnp.dot(y_lo, wr_ref[0, 0], preferred_element_type=F32)
               + jnp.dot(y_hi, wr_ref[0, 1], preferred_element_type=F32))

    lane = lax.broadcasted_iota(jnp.int32, lg.shape, 1).astype(F32)
    big = jnp.float32(1e9)
    is_g = (lane >= N_EXPERTS) & (lane < N_EXPERTS + N_GROUPS)
    gl = jnp.where(is_g, lg, -jnp.inf)
    gmax = jnp.max(gl, axis=-1, keepdims=True)
    gsel = jnp.min(jnp.where(gl == gmax, lane, big), axis=-1, keepdims=True) - N_EXPERTS
    g_gate = 1.0 / jnp.sum(jnp.where(is_g, jnp.exp(gl - gmax), 0.0), axis=-1, keepdims=True)
    lo = gsel * EXPERTS_PER_GROUP
    el = jnp.where((lane >= lo) & (lane < lo + EXPERTS_PER_GROUP), lg, -jnp.inf)
    e1 = jnp.max(el, axis=-1, keepdims=True)
    i1 = jnp.min(jnp.where(el == e1, lane, big), axis=-1, keepdims=True)
    el2 = jnp.where(lane == i1, -jnp.inf, el)
    e2 = jnp.max(el2, axis=-1, keepdims=True)
    i2 = jnp.min(jnp.where(el2 == e2, lane, big), axis=-1, keepdims=True)
    r = jnp.exp(e2 - e1)
    w1 = 1.0 / (1.0 + r)
    w2 = r / (1.0 + r)
    ids_ref[...] = jnp.where(lane == 0, i1, jnp.where(lane == 1, i2, 0.0)).astype(jnp.int32)
    gates_ref[...] = jnp.where(lane == 0, g_gate * w1, jnp.where(lane == 1, g_gate * w2, 0.0))


def _out_router(larr, d_out, n_out, w_out_b, xa, mods, norm2_w, wr):
    t, d = xa.shape
    half = d_out.shape[1]
    depth = norm2_w.shape[0]
    tile = lambda width: pl.BlockSpec((ROW_TILE, width), lambda i, l: (i, 0))
    return pl.pallas_call(
        _out_router_kernel,
        grid_spec=pltpu.PrefetchScalarGridSpec(
            num_scalar_prefetch=1, grid=(t // ROW_TILE,),
            in_specs=[tile(half), tile(half),
                      pl.BlockSpec((1, half, d), lambda i, l: (l[0], 0, 0)),
                      pl.BlockSpec((1, half, d), lambda i, l: (l[0], 1, 0)),
                      tile(d), _mod_spec(d, 2),
                      pl.BlockSpec((1, 1, d), lambda i, l: (l[0], 0, 0)),
                      _mod_spec(d, 3), _mod_spec(d, 4),
                      pl.BlockSpec((1, 2, d, HEAD_DIM), lambda i, l: (l[0], 0, 0, 0))],
            out_specs=[tile(d), tile(d), tile(HEAD_DIM), tile(HEAD_DIM)]),
        out_shape=[jax.ShapeDtypeStruct((t, d), F32), jax.ShapeDtypeStruct((t, d), F32),
                   jax.ShapeDtypeStruct((t, HEAD_DIM), jnp.int32),
                   jax.ShapeDtypeStruct((t, HEAD_DIM), F32)],
        compiler_params=_cparams(("arbitrary",)),
        name="out_router",
    )(larr, d_out, n_out, w_out_b, w_out_b, xa, mods, norm2_w.reshape(depth, 1, d), mods, mods, wr)


def _expert_kernel(l_ref, be_ref, st_ref, nu_ref, h_hbm, w1_ref, w3_ref, w2_ref, y_ref,
                   xbuf0, xbuf1, wb1, wb3, wb2, sem):
    b = pl.program_id(0)
    n_blocks = pl.num_programs(0)
    xbufs = (xbuf0, xbuf1)
    n_chunks = MOE_BLOCK // GATHER_CHUNK

    def rows_of(blk):
        return jnp.where(blk < n_blocks, nu_ref[jnp.minimum(blk, n_blocks - 1)], 0)

    def gather(blk, slot):
        n_rows = rows_of(blk)
        for c in range(n_chunks):
            @pl.when(c * GATHER_CHUNK < n_rows)
            def _():
                for r in range(c * GATHER_CHUNK, (c + 1) * GATHER_CHUNK):
                    tok = st_ref[blk * MOE_BLOCK + r]
                    pltpu.make_async_copy(h_hbm.at[pl.ds(tok, 1), :],
                                          xbufs[slot].at[pl.ds(r, 1), :], sem.at[slot]).start()

    def wait_rows(blk, slot):
        n_rows = rows_of(blk)
        for c in range(n_chunks):
            @pl.when(c * GATHER_CHUNK < n_rows)
            def _():
                pltpu.make_async_copy(h_hbm.at[pl.ds(0, GATHER_CHUNK), :],
                                      xbufs[slot].at[pl.ds(c * GATHER_CHUNK, GATHER_CHUNK), :],
                                      sem.at[slot]).wait()

    @pl.when(b == 0)
    def _():
        xbuf0[...] = jnp.zeros(xbuf0.shape, F32)
        xbuf1[...] = jnp.zeros(xbuf1.shape, F32)
        gather(0, 0)

    used = rows_of(b) > 0

    @pl.when(used & ((b == 0) | (be_ref[b] != be_ref[jnp.maximum(b - 1, 0)])))
    def _():
        wb1[...] = w1_ref[0, 0].astype(BF16)
        wb3[...] = w3_ref[0, 0].astype(BF16)
        wb2[...] = w2_ref[0, 0].astype(BF16)

    def run(slot):
        wait_rows(b, slot)
        gather(b + 1, 1 - slot)
        xb = xbufs[slot][...].astype(BF16)
        a = jnp.dot(xb, wb1[...], preferred_element_type=F32)
        g = jnp.dot(xb, wb3[...], preferred_element_type=F32)
        mid = (a / (1.0 + jnp.exp(-a))) * g
        y_ref[...] = jnp.dot(mid.astype(BF16), wb2[...], preferred_element_type=F32)

    for slot in range(2):
        pl.when(used & (b % 2 == slot))(functools.partial(run, slot))

    @pl.when(jnp.logical_not(used))
    def _():
        y_ref[...] = jnp.zeros(y_ref.shape, y_ref.dtype)


def _experts(larr, block_expert, slot_tok, block_rows, h2, w1, w3, w2):
    t, d = h2.shape
    ff = w1.shape[-1]
    n_blocks = block_expert.shape[0]
    return pl.pallas_call(
        _expert_kernel,
        grid_spec=pltpu.PrefetchScalarGridSpec(
            num_scalar_prefetch=4, grid=(n_blocks,),
            in_specs=[pl.BlockSpec(memory_space=pl.ANY),
                      pl.BlockSpec((1, 1, d, ff), lambda b, l, be, st, nu: (l[0], be[b], 0, 0)),
                      pl.BlockSpec((1, 1, d, ff), lambda b, l, be, st, nu: (l[0], be[b], 0, 0)),
                      pl.BlockSpec((1, 1, ff, d), lambda b, l, be, st, nu: (l[0], be[b], 0, 0))],
            out_specs=pl.BlockSpec((MOE_BLOCK, d), lambda b, l, be, st, nu: (b, 0)),
            scratch_shapes=[pltpu.VMEM((MOE_BLOCK, d), F32), pltpu.VMEM((MOE_BLOCK, d), F32),
                            pltpu.VMEM((d, ff), BF16), pltpu.VMEM((d, ff), BF16),
                            pltpu.VMEM((ff, d), BF16),
                            pltpu.SemaphoreType.DMA((2,))]),
        out_shape=jax.ShapeDtypeStruct((n_blocks * MOE_BLOCK, d), F32),
        compiler_params=_cparams(("arbitrary",)),
        name="experts",
    )(larr, block_expert, slot_tok, block_rows, h2, w1, w3, w2)


def _combine_kernel(l_ref, dest_ref, y_hbm, x_ref, gates_ref, g2_ref, nw_ref, sh_ref, sc_ref,
                    o_ref, h_ref, ybuf, sem):
    i = pl.program_id(0)
    is_ctx = i == 0
    buf = i % 2

    def gather(tile, dst):
        def issue(r, carry):
            for k in range(2):
                slot = dest_ref[(tile * ROW_TILE + r) * 2 + k]
                pltpu.make_async_copy(y_hbm.at[pl.ds(slot, 1), :],
                                      ybuf.at[dst, pl.ds(k * ROW_TILE + r, 1), :],
                                      sem.at[dst]).start()
            return carry
        lax.fori_loop(0, ROW_TILE, issue, 0)

    @pl.when(i == 0)
    def _():
        gather(0, 0)

    pltpu.make_async_copy(y_hbm.at[pl.ds(0, 2 * ROW_TILE), :], ybuf.at[buf], sem.at[buf]).wait()

    @pl.when(i + 1 < pl.num_programs(0))
    def _():
        gather(i + 1, 1 - buf)

    gates = gates_ref[...]
    y = (gates[:, 0:1] * ybuf[buf, 0:ROW_TILE, :]
         + gates[:, 1:2] * ybuf[buf, ROW_TILE:2 * ROW_TILE, :])
    x = x_ref[...] + _mod_row(g2_ref, is_ctx) * y
    o_ref[...] = x
    h = x * lax.rsqrt(jnp.mean(x * x, axis=-1, keepdims=True) + NORM_EPS) * nw_ref[0]
    h = h * (1.0 + _mod_row(sc_ref, is_ctx)) + _mod_row(sh_ref, is_ctx)
    h_ref[...] = h.astype(h_ref.dtype)


def _combine(larr, dest, y_slots, x_new, gates, mods, norm1_w):
    t, d = x_new.shape
    depth = norm1_w.shape[0]
    nxt = lambda l: jnp.minimum(l[0] + 1, depth - 1)
    tile = pl.BlockSpec((ROW_TILE, d), lambda i, l, dd: (i, 0))
    return pl.pallas_call(
        _combine_kernel,
        grid_spec=pltpu.PrefetchScalarGridSpec(
            num_scalar_prefetch=2, grid=(t // ROW_TILE,),
            in_specs=[pl.BlockSpec(memory_space=pl.ANY),
                      tile,
                      pl.BlockSpec((ROW_TILE, HEAD_DIM), lambda i, l, dd: (i, 0)),
                      pl.BlockSpec((1, 8, d), lambda i, l, dd: (l[0], 0, 5)),
                      pl.BlockSpec((1, 1, d), lambda i, l, dd: (nxt(l), 0, 0)),
                      pl.BlockSpec((1, 8, d), lambda i, l, dd: (nxt(l), 0, 0)),
                      pl.BlockSpec((1, 8, d), lambda i, l, dd: (nxt(l), 0, 1))],
            out_specs=[tile, tile],
            scratch_shapes=[pltpu.VMEM((2, 2 * ROW_TILE, d), F32),
                            pltpu.SemaphoreType.DMA((2,))]),
        out_shape=[jax.ShapeDtypeStruct((t, d), F32), jax.ShapeDtypeStruct((t, d), BF16)],
        compiler_params=_cparams(("arbitrary",)),
        name="combine",
    )(larr, dest, y_slots, x_new, gates, mods, norm1_w.reshape(depth, 1, d), mods, mods)


def _dispatch(ids):
    t = ids.shape[0]
    n_assign = 2 * t
    e_flat = ids.reshape(n_assign)
    onehot = (e_flat[:, None] == jnp.arange(N_EXPERTS, dtype=jnp.int32)[None, :]).astype(jnp.int32)
    csum = jnp.cumsum(onehot, axis=0)
    rank = jnp.sum(onehot * (csum - 1), axis=1)
    counts = csum[-1]
    padded = (counts + MOE_BLOCK - 1) // MOE_BLOCK * MOE_BLOCK
    padded_end = jnp.cumsum(padded)
    padded_start = padded_end - padded
    dest = (padded_start[e_flat] + rank).astype(jnp.int32)
    n_blocks = -(-n_assign // MOE_BLOCK) + N_EXPERTS
    tok = jnp.arange(n_assign, dtype=jnp.int32) // 2
    slot_tok = jnp.zeros((n_blocks * MOE_BLOCK,), jnp.int32).at[dest].set(tok)
    block_start = jnp.arange(n_blocks, dtype=jnp.int32) * MOE_BLOCK
    block_expert = jnp.clip(jnp.searchsorted(padded_end, block_start, side='right'),
                            0, N_EXPERTS - 1).astype(jnp.int32)
    filled_end = (padded_start + counts)[block_expert]
    block_rows = jnp.clip(filled_end - block_start, 0, MOE_BLOCK).astype(jnp.int32)
    return dest, slot_tok, block_expert, block_rows


def _diff_lane_fields():
    lane = np.arange(HEAD_DIM)
    part, m, half, f = lane // 64, (lane % 64) // 32, (lane % 32) // 16, lane % 16
    return part, m, half, f


def _rope_tables(ctx_len, seq):
    part, m, half, f = _diff_lane_fields()
    nf = HEAD_DIM // 8
    inv = ROPE_BASE ** (-jnp.arange(nf, dtype=F32) / nf)
    pos = jnp.arange(seq)
    prow = (pos // GRID_W).astype(F32)
    pcol = (pos % GRID_W).astype(F32)
    p = jnp.where(jnp.asarray(half)[None, :] == 0, prow[:, None], pcol[:, None])
    ang = p * inv[jnp.asarray(f)][None, :]
    sign = jnp.where(jnp.asarray(part) == 0, -1.0, 1.0).astype(F32)[None, :]
    cos = jnp.concatenate([jnp.ones((ctx_len, HEAD_DIM), F32), jnp.cos(ang)], axis=0)
    sin = jnp.concatenate([jnp.zeros((ctx_len, HEAD_DIM), F32), jnp.sin(ang) * sign], axis=0)
    return cos, sin


def _permute_diff_cols(w, n_heads):
    lead = w.shape[:-1]
    w = w.reshape(lead + (n_heads, 2, 2, 2, 16))
    nd = len(lead)
    w = jnp.transpose(w, tuple(range(nd)) + (nd, nd + 3, nd + 1, nd + 2, nd + 4))
    return w.reshape(lead + (n_heads * HEAD_DIM,))


def _qk_weight_table(diff_q_norm, diff_k_norm, na_q_norm, na_k_norm):
    part, m, half, f = _diff_lane_fields()
    src = half * 32 + part * 16 + f
    dqk = HEAD_DIM // 2
    dq = diff_q_norm[:, src] * (dqk ** -0.5 * LOG2E)
    dk = diff_k_norm[:, src]
    nq = na_q_norm * (HEAD_DIM ** -0.5 * LOG2E)
    rows = jnp.stack([dq, dk, nq, na_k_norm], axis=1)
    return jnp.concatenate([rows, jnp.zeros_like(rows)], axis=1)


def kernel(x, c, ctx, c_ctx, ada_w, ada_b, norm1_w, norm2_w, w_in, w_out, diff_q_norm, diff_k_norm,
           diff_lq1, diff_lk1, diff_lq2, diff_lk2, diff_subln, na_q_norm, na_k_norm, na_rpb,
           moe_w_group, moe_w_expert, moe_w1, moe_w3, moe_w2):
    _, seq, d = x.shape
    ctx_len = ctx.shape[1]
    depth = ada_w.shape[0]
    assert ctx_len == ROW_TILE and seq % ROW_TILE == 0 and x.shape[0] == 1
    n_heads = d // HEAD_DIM
    n_diff = n_heads // 2
    n_na = n_heads - n_diff
    seg = n_diff * HEAD_DIM
    t = ctx_len + seq
    n_tiles = t // ROW_TILE

    xa = jnp.concatenate([ctx[0], x[0]], axis=0)
    c2 = jnp.zeros((8, d), F32).at[0].set(c[0]).at[1].set(c_ctx)
    mods = _adaln(c2, ada_w, ada_b)

    w_in_b = w_in.astype(BF16)
    w_q = _permute_diff_cols(w_in_b[..., :seg], n_diff)
    w_k = _permute_diff_cols(w_in_b[..., seg:2 * seg], n_diff)
    w_out_b = w_out.astype(BF16)
    qkw = _qk_weight_table(diff_q_norm, diff_k_norm, na_q_norm, na_k_norm)
    cos_t, sin_t = _rope_tables(ctx_len, seq)
    na_slabs, na_slab_idx = _na_bias_slabs(na_rpb, n_tiles)

    lam_init = jnp.asarray([0.8 - 0.6 * math.exp(-0.3 * l) for l in range(depth)], F32)
    lam = (jnp.exp(jnp.sum(diff_lq1 * diff_lk1, axis=-1)) - jnp.exp(jnp.sum(diff_lq2 * diff_lk2, axis=-1))
           + lam_init)
    bound = (HEAD_DIM // 2) * jnp.max(jnp.abs(qkw[:, 0]), axis=-1) * jnp.max(jnp.abs(qkw[:, 1]), axis=-1)
    bound = bound * 1.02 + 0.01
    bound = jnp.where(bound <= MAX_STATIC_SHIFT, bound, -1.0)
    lam_tab = jnp.stack([lam, 1.0 - lam_init, bound], axis=1).reshape(DIFF_TAB * depth)

    wr = jnp.concatenate([moe_w_expert, moe_w_group,
                          jnp.zeros((depth, d, HEAD_DIM - N_EXPERTS - N_GROUPS), F32)], axis=-1)
    wr_hi = wr.astype(BF16)
    wr_lo = (wr - wr_hi.astype(F32)).astype(BF16)
    wr_split = jnp.stack([wr_hi, wr_lo], axis=1)

    h = _norm_mod(jnp.zeros((1,), jnp.int32), xa, norm1_w, mods)
    for l in range(depth):
        larr = jnp.full((1,), l, jnp.int32)
        proj = _in_proj(larr, h, w_q, w_k, w_in_b, qkw, cos_t, sin_t)
        d_out = _diff_attn(larr, lam_tab, proj, diff_subln, n_diff, ctx_len)
        n_out = _na_attn(larr, proj, na_slabs, na_slab_idx, n_na, n_diff)
        x_new, h2, ids, gates = _out_router(larr, d_out, n_out, w_out_b, xa, mods, norm2_w, wr_split)
        dest, slot_tok, block_expert, block_rows = _dispatch(ids[:, :2])
        y_slots = _experts(larr, block_expert, slot_tok, block_rows, h2, moe_w1, moe_w3, moe_w2)
        xa, h = _combine(larr, dest, y_slots, x_new, gates, mods, norm1_w)
    return xa[ctx_len:][None]
```

```python
import functools
import math

import numpy as np
import jax
import jax.numpy as jnp
from jax import lax
from jax.experimental import pallas as pl
from jax.experimental.pallas import tpu as pltpu

F32 = jnp.float32
BF16 = jnp.bfloat16

GRID_W = 64
HEAD_DIM = 128
WIN_R = 8
WIN_C = 16
ROPE_BASE = 10000.0
N_GROUPS = 4
EXPERTS_PER_GROUP = 8
N_EXPERTS = N_GROUPS * EXPERTS_PER_GROUP
NORM_EPS = 1e-6
NEG_INF = -1e30
LOG2E = 1.4426950408889634

ROW_TILE = 256
NA_ROWS = ROW_TILE // GRID_W
NA_WIN_TILES = 3
MOE_BLOCK = 256
GATHER_CHUNK = 32
ADA_TN = 1536
IN_PROJ_TM = (384, 256)
DIFF_TK = 512
DIFF_TK_ALL = (1408, 768, 640, 512, 256)
VMEM_LIMIT = 56 * 1024 * 1024


def _cparams(sem, vmem=VMEM_LIMIT):
    return pltpu.CompilerParams(dimension_semantics=sem, vmem_limit_bytes=vmem)


def _nt_dot(a, b):
    return lax.dot_general(a, b, (((1,), (1,)), ((), ())), preferred_element_type=F32)


def _adaln_kernel(c_ref, w_ref, b_ref, o_ref):
    c = c_ref[...]
    a = c / (1.0 + jnp.exp(-c))
    o_ref[0] = jnp.dot(a.astype(BF16), w_ref[0].astype(BF16),
                       preferred_element_type=F32) + b_ref[0]


def _adaln(c2, ada_w, ada_b):
    depth, d, six_d = ada_w.shape
    tn = ADA_TN
    return pl.pallas_call(
        _adaln_kernel,
        grid=(depth, six_d // tn),
        in_specs=[pl.BlockSpec((8, d), lambda l, j: (0, 0)),
                  pl.BlockSpec((1, d, tn), lambda l, j: (l, 0, j)),
                  pl.BlockSpec((1, 1, tn), lambda l, j: (l, 0, j))],
        out_specs=pl.BlockSpec((1, 8, tn), lambda l, j: (l, 0, j)),
        out_shape=jax.ShapeDtypeStruct((depth, 8, six_d), F32),
        compiler_params=_cparams(("arbitrary", "arbitrary")),
        name="adaln",
    )(c2, ada_w, ada_b.reshape(depth, 1, six_d))


def _mod_spec(d, chunk):
    return pl.BlockSpec((1, 8, d), lambda i, l: (l[0], 0, chunk))


def _mod_row(ref, is_ctx):
    return jnp.where(is_ctx, ref[0, 1:2, :], ref[0, 0:1, :])


def _norm_mod_kernel(l_ref, x_ref, nw_ref, sh_ref, sc_ref, o_ref):
    is_ctx = pl.program_id(0) == 0
    x = x_ref[...]
    y = x * lax.rsqrt(jnp.mean(x * x, axis=-1, keepdims=True) + NORM_EPS) * nw_ref[0]
    y = y * (1.0 + _mod_row(sc_ref, is_ctx)) + _mod_row(sh_ref, is_ctx)
    o_ref[...] = y.astype(o_ref.dtype)


def _norm_mod(larr, xa, norm_w, mods):
    t, d = xa.shape
    depth = norm_w.shape[0]
    return pl.pallas_call(
        _norm_mod_kernel,
        grid_spec=pltpu.PrefetchScalarGridSpec(
            num_scalar_prefetch=1, grid=(t // ROW_TILE,),
            in_specs=[pl.BlockSpec((ROW_TILE, d), lambda i, l: (i, 0)),
                      pl.BlockSpec((1, 1, d), lambda i, l: (l[0], 0, 0)),
                      _mod_spec(d, 0), _mod_spec(d, 1)],
            out_specs=pl.BlockSpec((ROW_TILE, d), lambda i, l: (i, 0))),
        out_shape=jax.ShapeDtypeStruct((t, d), BF16),
        compiler_params=_cparams(("arbitrary",)),
        name="norm_mod",
    )(larr, xa, norm_w.reshape(depth, 1, d), mods, mods)


def _in_proj_kernel(l_ref, x_ref, wq_ref, wk_ref, wm_ref, wh_ref, qkw_ref, cos_ref, sin_ref, o_ref,
                    acc_ref, *, seg):
    n_chunks = seg // HEAD_DIM
    x = x_ref[...]
    a = lax.broadcasted_iota(jnp.int32, (HEAD_DIM, HEAD_DIM), 0)
    b = lax.broadcasted_iota(jnp.int32, (HEAD_DIM, HEAD_DIM), 1)
    map_mat = ((a & 32) == (b & 32)).astype(BF16)
    head_mat = jnp.ones((HEAD_DIM, HEAD_DIM), BF16)
    weights = (wq_ref[0], wk_ref[0], wm_ref[0, :, 0:seg], wm_ref[0, :, seg:2 * seg],
               wh_ref[0, :, 0:seg], wh_ref[0, :, seg:2 * seg])
    norm_row = {0: 0, 1: 1, 3: 2, 4: 3}

    def normed(buf, c, group_mat, inv_n, wrow):
        xc = acc_ref[buf, :, c * HEAD_DIM:(c + 1) * HEAD_DIM]
        ss = jnp.dot((xc * xc).astype(BF16), group_mat, preferred_element_type=F32)
        return xc * lax.rsqrt(ss * inv_n + NORM_EPS) * wrow

    for j in range(6):
        buf = j % 2
        acc_ref[buf] = jnp.dot(x, weights[j], preferred_element_type=F32)
        for c in range(n_chunks):
            cols = slice(j * seg + c * HEAD_DIM, j * seg + (c + 1) * HEAD_DIM)
            if j in (2, 5):
                y = acc_ref[buf, :, c * HEAD_DIM:(c + 1) * HEAD_DIM]
            elif j < 2:
                wrow = qkw_ref[0, norm_row[j]:norm_row[j] + 1, :]
                y = normed(buf, c, map_mat, 2.0 / HEAD_DIM, wrow)
                y = y * cos_ref[...] + pltpu.roll(y, HEAD_DIM // 2, 1) * sin_ref[...]
            else:
                wrow = qkw_ref[0, norm_row[j]:norm_row[j] + 1, :]
                y = normed(buf, c, head_mat, 1.0 / HEAD_DIM, wrow)
            o_ref[:, cols] = y.astype(o_ref.dtype)


def _pick_tile(n, candidates):
    for c in candidates:
        if n % c == 0:
            return c
    raise ValueError(f"no tile for {n}")


def _in_proj(larr, h, w_q, w_k, w_in_b, qkw, cos_t, sin_t):
    t, d = h.shape
    seg = w_q.shape[2]
    tm = _pick_tile(t, IN_PROJ_TM)
    once = pl.Buffered(1)

    def weight_spec(width, col):
        return pl.BlockSpec((1, d, width), lambda i, l: (l[0], 0, col), pipeline_mode=once)

    return pl.pallas_call(
        functools.partial(_in_proj_kernel, seg=seg),
        grid_spec=pltpu.PrefetchScalarGridSpec(
            num_scalar_prefetch=1, grid=(t // tm,),
            in_specs=[pl.BlockSpec((tm, d), lambda i, l: (i, 0)),
                      weight_spec(seg, 0), weight_spec(seg, 0),
                      weight_spec(2 * seg, 1), weight_spec(2 * seg, 2),
                      pl.BlockSpec((1, 8, HEAD_DIM), lambda i, l: (l[0], 0, 0)),
                      pl.BlockSpec((tm, HEAD_DIM), lambda i, l: (i, 0)),
                      pl.BlockSpec((tm, HEAD_DIM), lambda i, l: (i, 0))],
            out_specs=pl.BlockSpec((tm, 6 * seg), lambda i, l: (i, 0)),
            scratch_shapes=[pltpu.VMEM((2, tm, seg), F32)]),
        out_shape=jax.ShapeDtypeStruct((t, 6 * seg), BF16),
        compiler_params=_cparams(("arbitrary",)),
        name="in_proj",
    )(larr, h, w_q, w_k, w_in_b, w_in_b, qkw, cos_t, sin_t)


DIFF_HEADS_PER_STEP = (4, 2, 1)
DIFF_TAB = 3
MAX_STATIC_SHIFT = 60.0


def _diff_attn_kernel(l_ref, lam_ref, q_ref, k_ref, v_ref, sw_ref, o_ref, m_ref, s_ref, part_ref,
                      acc_ref, *, ctx_len, tk, tk_all, heads):
    i = pl.program_id(1)
    layer = l_ref[0]
    tq = q_ref.shape[0]
    n_lat = (k_ref.shape[0] - ctx_len) // tk
    lam = lam_ref[DIFF_TAB * layer]
    out_scale = lam_ref[DIFF_TAB * layer + 1]
    bound = lam_ref[DIFF_TAB * layer + 2]
    lane = lax.broadcasted_iota(jnp.int32, (1, HEAD_DIM), 1)
    in_map0 = (lane & 32) == 0

    def head_cols(hh):
        return slice(hh * HEAD_DIM, (hh + 1) * HEAD_DIM)

    def masked_q(hh):
        q = q_ref[:, head_cols(hh)]
        return (jnp.where(in_map0, q, jnp.zeros_like(q)), jnp.where(in_map0, jnp.zeros_like(q), q))

    def finish(hh, l0, l1):
        o = acc_ref[hh, 0] / l0 - lam * (acc_ref[hh, 1] / l1)
        o = o * lax.rsqrt(jnp.mean(o * o, axis=-1, keepdims=True) + NORM_EPS) * sw_ref[0] * out_scale
        o_ref[:, head_cols(hh)] = o.astype(o_ref.dtype)

    @pl.when(bound >= 0.0)
    def _():
        def attend(hh, n_keys, chunk):
            hs = head_cols(hh)
            q_both = jnp.concatenate(masked_q(hh), axis=0)
            for lo in range(0, n_keys, chunk):
                p = jnp.exp2(_nt_dot(q_both, k_ref[lo:lo + chunk, hs]) - bound)
                prt = p[:, 0:HEAD_DIM]
                for j in range(1, chunk // HEAD_DIM):
                    prt = prt + p[:, j * HEAD_DIM:(j + 1) * HEAD_DIM]
                pv = jnp.dot(p.astype(BF16), v_ref[lo:lo + chunk, hs], preferred_element_type=F32)
                for mi in range(2):
                    if lo == 0:
                        part_ref[hh, mi] = prt[mi * tq:(mi + 1) * tq]
                        acc_ref[hh, mi] = pv[mi * tq:(mi + 1) * tq]
                    else:
                        part_ref[hh, mi] += prt[mi * tq:(mi + 1) * tq]
                        acc_ref[hh, mi] += pv[mi * tq:(mi + 1) * tq]

        def run(n_keys, chunk):
            for hh in range(heads):
                attend(hh, n_keys, chunk)
                finish(hh, jnp.sum(part_ref[hh, 0], axis=-1, keepdims=True),
                       jnp.sum(part_ref[hh, 1], axis=-1, keepdims=True))

        @pl.when(i == 0)
        def _():
            run(ctx_len, ctx_len)

        @pl.when(i > 0)
        def _():
            run(k_ref.shape[0], tk_all)

    @pl.when(bound < 0.0)
    def _():
        for hh in range(heads):
            hs = head_cols(hh)
            qs = masked_q(hh)
            m_ref[...] = jnp.full(m_ref.shape, -jnp.inf, F32)
            s_ref[...] = jnp.zeros(s_ref.shape, F32)
            acc_ref[hh] = jnp.zeros(acc_ref.shape[1:], F32)

            def step(kc, vc, hh=hh, qs=qs):
                for mi in range(2):
                    s = _nt_dot(qs[mi], kc)
                    m_prev = m_ref[mi]
                    m_new = jnp.maximum(m_prev, jnp.max(s, axis=-1, keepdims=True))
                    alpha = jnp.exp2(m_prev - m_new)
                    p = jnp.exp2(s - m_new)
                    s_ref[mi] = alpha * s_ref[mi] + jnp.sum(p, axis=-1, keepdims=True)
                    acc_ref[hh, mi] = alpha * acc_ref[hh, mi] + jnp.dot(
                        p.astype(BF16), vc, preferred_element_type=F32)
                    m_ref[mi] = m_new

            step(k_ref[0:ctx_len, hs], v_ref[0:ctx_len, hs])

            def body(c, carry, step=step, hs=hs):
                off = pl.multiple_of(ctx_len + c * tk, math.gcd(ctx_len, tk))
                step(k_ref[pl.ds(off, tk), hs], v_ref[pl.ds(off, tk), hs])
                return carry
            lax.fori_loop(0, jnp.where(i == 0, 0, n_lat), body, 0)
            finish(hh, s_ref[0], s_ref[1])


def _diff_attn(larr, lam_tab, proj, subln, n_heads, ctx_len):
    t = proj.shape[0]
    depth = subln.shape[0]
    tq = ROW_TILE
    tk = _pick_tile(t - ctx_len, (DIFF_TK, 256))
    heads = _pick_tile(n_heads, DIFF_HEADS_PER_STEP)
    n_hb = n_heads // heads
    wb = heads * HEAD_DIM
    return pl.pallas_call(
        functools.partial(_diff_attn_kernel, ctx_len=ctx_len, tk=tk,
                          tk_all=_pick_tile(t, DIFF_TK_ALL), heads=heads),
        grid_spec=pltpu.PrefetchScalarGridSpec(
            num_scalar_prefetch=1, grid=(n_hb, t // tq),
            in_specs=[pl.BlockSpec(memory_space=pltpu.SMEM),
                      pl.BlockSpec((tq, wb), lambda h, i, l: (i, h)),
                      pl.BlockSpec((t, wb), lambda h, i, l: (0, n_hb + h)),
                      pl.BlockSpec((t, wb), lambda h, i, l: (0, 2 * n_hb + h)),
                      pl.BlockSpec((1, 1, HEAD_DIM), lambda h, i, l: (l[0], 0, 0))],
            out_specs=pl.BlockSpec((tq, wb), lambda h, i, l: (i, h)),
            scratch_shapes=[pltpu.VMEM((2, tq, 1), F32), pltpu.VMEM((2, tq, 1), F32),
                            pltpu.VMEM((heads, 2, tq, HEAD_DIM), F32),
                            pltpu.VMEM((heads, 2, tq, HEAD_DIM), F32)]),
        out_shape=jax.ShapeDtypeStruct((t, n_heads * HEAD_DIM), BF16),
        compiler_params=_cparams(("arbitrary", "arbitrary")),
        name="diff_attn",
    )(larr, lam_tab, proj, proj, proj, subln.reshape(depth, 1, HEAD_DIM))


def _na_kernel(l_ref, idx_ref, q_ref, kc_ref, k0_ref, k1_ref, k2_ref, vc_ref, v0_ref, v1_ref, v2_ref,
               slab_ref, o_ref, *, n_heads):
    kw = (k0_ref, k1_ref, k2_ref)
    vw = (v0_ref, v1_ref, v2_ref)
    g = pl.program_id(0)
    n_tiles = pl.num_programs(0)
    pat = jnp.where(g == 0, 0, jnp.where(g == 1, 1, jnp.where(g == n_tiles - 1, 3, 2)))
    n_kr = NA_WIN_TILES * NA_ROWS
    low_lanes = lax.broadcasted_iota(jnp.int32, (GRID_W, 2 * GRID_W), 1) < GRID_W

    def bias_tile(h, j):
        rows = []
        for a in range(NA_ROWS):
            pairs = []
            for b in range(NA_ROWS * j, NA_ROWS * (j + 1), 2):
                i0 = idx_ref[(pat * NA_ROWS + a) * n_kr + b]
                i1 = idx_ref[(pat * NA_ROWS + a) * n_kr + b + 1]
                pairs.append(jnp.where(low_lanes, slab_ref[0, h, i0], slab_ref[0, h, i1]))
            rows.append(jnp.concatenate(pairs, axis=1))
        return jnp.concatenate(rows, axis=0)

    for h in range(n_heads):
        hs = slice(h * HEAD_DIM, (h + 1) * HEAD_DIM)
        qh = q_ref[:, hs]
        s_c = _nt_dot(qh, kc_ref[:, hs])
        s_w = [_nt_dot(qh, kw[j][:, hs]) + bias_tile(h, j) for j in range(NA_WIN_TILES)]
        m = jnp.max(s_c, axis=-1, keepdims=True)
        for s in s_w:
            m = jnp.maximum(m, jnp.max(s, axis=-1, keepdims=True))
        p_c = jnp.exp2(s_c - m)
        denom = jnp.sum(p_c, axis=-1, keepdims=True)
        o = jnp.dot(p_c.astype(BF16), vc_ref[:, hs], preferred_element_type=F32)
        for j in range(NA_WIN_TILES):
            p = jnp.exp2(s_w[j] - m)
            denom = denom + jnp.sum(p, axis=-1, keepdims=True)
            o = o + jnp.dot(p.astype(BF16), vw[j][:, hs], preferred_element_type=F32)
        o_ref[:, hs] = (o / denom).astype(o_ref.dtype)


def _na_win_base(g, n_tiles):
    return 1 + jnp.clip(g - 2, 0, n_tiles - 1 - NA_WIN_TILES)


def _na_attn(larr, proj, slabs, slab_idx, n_heads, n_diff_heads):
    t = proj.shape[0]
    n_tiles = t // ROW_TILE
    w = n_heads * HEAD_DIM
    qcol = 3 * n_diff_heads * HEAD_DIM // w
    blk = (ROW_TILE, w)

    def win_spec(col, j):
        return pl.BlockSpec(blk, lambda g, l, ix: (_na_win_base(g, n_tiles) + j, col))

    in_specs = [pl.BlockSpec(blk, lambda g, l, ix: (g, qcol)),
                pl.BlockSpec(blk, lambda g, l, ix: (0, qcol + 1))]
    in_specs += [win_spec(qcol + 1, j) for j in range(NA_WIN_TILES)]
    in_specs += [pl.BlockSpec(blk, lambda g, l, ix: (0, qcol + 2))]
    in_specs += [win_spec(qcol + 2, j) for j in range(NA_WIN_TILES)]
    in_specs += [pl.BlockSpec((1,) + slabs.shape[1:], lambda g, l, ix: (l[0], 0, 0, 0, 0))]
    return pl.pallas_call(
        functools.partial(_na_kernel, n_heads=n_heads),
        grid_spec=pltpu.PrefetchScalarGridSpec(
            num_scalar_prefetch=2, grid=(n_tiles,),
            in_specs=in_specs,
            out_specs=pl.BlockSpec(blk, lambda g, l, ix: (g, 0))),
        out_shape=jax.ShapeDtypeStruct((t, w), BF16),
        compiler_params=_cparams(("arbitrary",)),
        name="na_attn",
    )(larr, slab_idx, *([proj] * 9), slabs)


def _na_bias_slabs(rpb, n_tiles):
    depth, n_h = rpb.shape[:2]
    rows = (n_tiles - 1) * NA_ROWS
    kr = min(WIN_R, rows)
    n_dr = 2 * WIN_R - 1
    n_kr = NA_WIN_TILES * NA_ROWS
    qc = np.arange(GRID_W)[:, None]
    kc = np.arange(GRID_W)[None, :]
    cs = np.clip(qc - WIN_C // 2, 0, GRID_W - WIN_C)
    col_valid = (kc >= cs) & (kc < cs + WIN_C)
    pad = GRID_W - WIN_C
    row = jnp.concatenate([jnp.repeat(rpb[..., :1], pad, axis=-1), rpb,
                           jnp.repeat(rpb[..., -1:], pad, axis=-1)], axis=-1)
    slabs = jnp.stack([row[..., GRID_W - 1 - c:2 * GRID_W - 1 - c] for c in range(GRID_W)], axis=-2)
    slabs = jnp.where(col_valid[None, None, None], slabs, NEG_INF) * LOG2E
    masked = jnp.full((depth, n_h, 1, GRID_W, GRID_W), NEG_INF * LOG2E, F32)
    slabs = jnp.concatenate([slabs, masked], axis=2)
    slab_idx = np.full((4, NA_ROWS, n_kr), n_dr, np.int32)
    for p, gl in enumerate((0, 1, n_tiles - 2)):
        qr = NA_ROWS * gl + np.arange(NA_ROWS)[:, None]
        win_start = NA_ROWS * int(np.clip(gl - 1, 0, n_tiles - 1 - NA_WIN_TILES))
        krow = win_start + np.arange(n_kr)[None, :]
        rs = np.clip(qr - kr // 2, 0, rows - kr)
        row_valid = (krow >= rs) & (krow < rs + kr)
        slab_idx[p + 1] = np.where(row_valid, krow - qr + (WIN_R - 1), n_dr)
    slabs = jnp.concatenate([slabs, slabs], axis=-1)
    return slabs, jnp.asarray(slab_idx.reshape(-1))


def _out_router_kernel(l_ref, d_ref, n_ref, wa_ref, wb_ref, x_ref, g1_ref, nw_ref, sh_ref, sc_ref,
                       wr_ref, xo_ref, h_ref, ids_ref, gates_ref, sorted_ref, cnt_ref):
    is_ctx = pl.program_id(0) == 0
    acc = jnp.dot(d_ref[...], wa_ref[0], preferred_element_type=F32)
    acc = acc + jnp.dot(n_ref[...], wb_ref[0], preferred_element_type=F32)
    x = x_ref[...] + _mod_row(g1_ref, is_ctx) * acc
    xo_ref[...] = x
    y = x * lax.rsqrt(jnp.mean(x * x, axis=-1, keepdims=True) + NORM_EPS) * nw_ref[0]
    y = y * (1.0 + _mod_row(sc_ref, is_ctx)) + _mod_row(sh_ref, is_ctx)
    h_ref[...] = y

    y_hi = y.astype(BF16)
    y_lo = (y - y_hi.astype(F32)).astype(BF16)
    lg = jnp.dot(y_hi, wr_ref[0, 0], preferred_element_type=F32)
    lg = lg + (jnp.dot(y_lo, wr_ref[0, 0], preferred_element_type=F32)
               + jnp.dot(y_hi, wr_ref[0, 1], preferred_element_type=F32))

    lane = lax.broadcasted_iota(jnp.int32, lg.shape, 1).astype(F32)
    big = jnp.float32(1e9)
    is_g = (lane >= N_EXPERTS) & (lane < N_EXPERTS + N_GROUPS)
    gl = jnp.where(is_g, lg, -jnp.inf)
    gmax = jnp.max(gl, axis=-1, keepdims=True)
    gsel = jnp.min(jnp.where(gl == gmax, lane, big), axis=-1, keepdims=True) - N_EXPERTS
    g_gate = 1.0 / jnp.sum(jnp.where(is_g, jnp.exp(gl - gmax), 0.0), axis=-1, keepdims=True)
    lo = gsel * EXPERTS_PER_GROUP
    el = jnp.where((lane >= lo) & (lane < lo + EXPERTS_PER_GROUP), lg, -jnp.inf)
    e1 = jnp.max(el, axis=-1, keepdims=True)
    i1 = jnp.min(jnp.where(el == e1, lane, big), axis=-1, keepdims=True)
    el2 = jnp.where(lane == i1, -jnp.inf, el)
    e2 = jnp.max(el2, axis=-1, keepdims=True)
    i2 = jnp.min(jnp.where(el2 == e2, lane, big), axis=-1, keepdims=True)
    r = jnp.exp(e2 - e1)
    w1 = 1.0 / (1.0 + r)
    w2 = r / (1.0 + r)
    gates_ref[...] = jnp.where(lane == 0, g_gate * w1, jnp.where(lane == 1, g_gate * w2, 0.0))

    tm = lg.shape[0]
    pick1 = (lane == i1).astype(F32)
    pick2 = (lane == i2).astype(F32)
    uses = pick1 + pick2
    row_i = lax.broadcasted_iota(jnp.int32, (tm, tm), 0)
    col_i = lax.broadcasted_iota(jnp.int32, (tm, tm), 1)
    earlier = (col_i < row_i).astype(BF16)
    before = jnp.dot(earlier, uses.astype(BF16), preferred_element_type=F32)
    r1 = jnp.sum(pick1 * before, axis=-1, keepdims=True)
    r2 = jnp.sum(pick2 * before, axis=-1, keepdims=True)
    counts = jnp.sum(uses, axis=0, keepdims=True)
    le = lax.broadcasted_iota(jnp.int32, (HEAD_DIM, HEAD_DIM), 0)
    ge = lax.broadcasted_iota(jnp.int32, (HEAD_DIM, HEAD_DIM), 1)
    lower_experts = (le < ge).astype(BF16)
    starts = jnp.dot(jnp.broadcast_to(counts, (8, HEAD_DIM)).astype(BF16), lower_experts,
                     preferred_element_type=F32)[0:1]
    pos1 = jnp.sum(pick1 * starts, axis=-1, keepdims=True) + r1
    pos2 = jnp.sum(pick2 * starts, axis=-1, keepdims=True) + r2
    place = lax.broadcasted_iota(jnp.int32, (tm, 2 * tm), 1).astype(F32)
    holds = ((place == pos1) | (place == pos2)).astype(BF16)
    tok_local = lax.broadcasted_iota(jnp.int32, (8, tm), 1).astype(BF16)
    sorted_local = jnp.dot(tok_local, holds, preferred_element_type=F32)
    sorted_ref[0] = sorted_local.astype(jnp.int32) + pl.program_id(0) * tm
    cnt_ref[0] = jnp.broadcast_to(counts, (8, HEAD_DIM)).astype(jnp.int32)
    ids_ref[...] = jnp.where(lane == 0, i1, jnp.where(lane == 1, i2, jnp.where(
        lane == 2, r1, jnp.where(lane == 3, r2, 0.0)))).astype(jnp.int32)


def _out_router(larr, d_out, n_out, w_out_b, xa, mods, norm2_w, wr):
    t, d = xa.shape
    half = d_out.shape[1]
    depth = norm2_w.shape[0]
    tile = lambda width: pl.BlockSpec((ROW_TILE, width), lambda i, l: (i, 0))
    return pl.pallas_call(
        _out_router_kernel,
        grid_spec=pltpu.PrefetchScalarGridSpec(
            num_scalar_prefetch=1, grid=(t // ROW_TILE,),
            in_specs=[tile(half), tile(half),
                      pl.BlockSpec((1, half, d), lambda i, l: (l[0], 0, 0)),
                      pl.BlockSpec((1, half, d), lambda i, l: (l[0], 1, 0)),
                      tile(d), _mod_spec(d, 2),
                      pl.BlockSpec((1, 1, d), lambda i, l: (l[0], 0, 0)),
                      _mod_spec(d, 3), _mod_spec(d, 4),
                      pl.BlockSpec((1, 2, d, HEAD_DIM), lambda i, l: (l[0], 0, 0, 0))],
            out_specs=[tile(d), tile(d), tile(HEAD_DIM), tile(HEAD_DIM),
                       pl.BlockSpec((1, 8, 2 * ROW_TILE), lambda i, l: (i, 0, 0)),
                       pl.BlockSpec((1, 8, HEAD_DIM), lambda i, l: (i, 0, 0))]),
        out_shape=[jax.ShapeDtypeStruct((t, d), F32), jax.ShapeDtypeStruct((t, d), F32),
                   jax.ShapeDtypeStruct((t, HEAD_DIM), jnp.int32),
                   jax.ShapeDtypeStruct((t, HEAD_DIM), F32),
                   jax.ShapeDtypeStruct((t // ROW_TILE, 8, 2 * ROW_TILE), jnp.int32),
                   jax.ShapeDtypeStruct((t // ROW_TILE, 8, HEAD_DIM), jnp.int32)],
        compiler_params=_cparams(("arbitrary",)),
        name="out_router",
    )(larr, d_out, n_out, w_out_b, w_out_b, xa, mods, norm2_w.reshape(depth, 1, d), mods, mods, wr)


def _expert_kernel(l_ref, be_ref, st_ref, nu_ref, h_hbm, w1_ref, w3_ref, w2_ref, y_ref,
                   xbuf0, xbuf1, wb1, wb3, wb2, sem):
    b = pl.program_id(0)
    n_blocks = pl.num_programs(0)
    xbufs = (xbuf0, xbuf1)
    n_chunks = MOE_BLOCK // GATHER_CHUNK

    def rows_of(blk):
        return jnp.where(blk < n_blocks, nu_ref[jnp.minimum(blk, n_blocks - 1)], 0)

    def gather(blk, slot):
        n_rows = rows_of(blk)
        for c in range(n_chunks):
            @pl.when(c * GATHER_CHUNK < n_rows)
            def _():
                for r in range(c * GATHER_CHUNK, (c + 1) * GATHER_CHUNK):
                    tok = st_ref[blk * MOE_BLOCK + r]
                    pltpu.make_async_copy(h_hbm.at[pl.ds(tok, 1), :],
                                          xbufs[slot].at[pl.ds(r, 1), :], sem.at[slot]).start()

    def wait_rows(blk, slot):
        n_rows = rows_of(blk)
        for c in range(n_chunks):
            @pl.when(c * GATHER_CHUNK < n_rows)
            def _():
                pltpu.make_async_copy(h_hbm.at[pl.ds(0, GATHER_CHUNK), :],
                                      xbufs[slot].at[pl.ds(c * GATHER_CHUNK, GATHER_CHUNK), :],
                                      sem.at[slot]).wait()

    @pl.when(b == 0)
    def _():
        xbuf0[...] = jnp.zeros(xbuf0.shape, F32)
        xbuf1[...] = jnp.zeros(xbuf1.shape, F32)
        gather(0, 0)

    used = rows_of(b) > 0

    @pl.when(used & ((b == 0) | (be_ref[b] != be_ref[jnp.maximum(b - 1, 0)])))
    def _():
        wb1[...] = w1_ref[0, 0].astype(BF16)
        wb3[...] = w3_ref[0, 0].astype(BF16)
        wb2[...] = w2_ref[0, 0].astype(BF16)

    def run(slot):
        wait_rows(b, slot)
        gather(b + 1, 1 - slot)
        xb = xbufs[slot][...].astype(BF16)
        a = jnp.dot(xb, wb1[...], preferred_element_type=F32)
        g = jnp.dot(xb, wb3[...], preferred_element_type=F32)
        mid = (a / (1.0 + jnp.exp(-a))) * g
        y_ref[...] = jnp.dot(mid.astype(BF16), wb2[...], preferred_element_type=F32)

    for slot in range(2):
        pl.when(used & (b % 2 == slot))(functools.partial(run, slot))

    @pl.when(jnp.logical_not(used))
    def _():
        y_ref[...] = jnp.zeros(y_ref.shape, y_ref.dtype)


def _experts(larr, block_expert, slot_tok, block_rows, h2, w1, w3, w2):
    t, d = h2.shape
    ff = w1.shape[-1]
    n_blocks = block_expert.shape[0]
    return pl.pallas_call(
        _expert_kernel,
        grid_spec=pltpu.PrefetchScalarGridSpec(
            num_scalar_prefetch=4, grid=(n_blocks,),
            in_specs=[pl.BlockSpec(memory_space=pl.ANY),
                      pl.BlockSpec((1, 1, d, ff), lambda b, l, be, st, nu: (l[0], be[b], 0, 0)),
                      pl.BlockSpec((1, 1, d, ff), lambda b, l, be, st, nu: (l[0], be[b], 0, 0)),
                      pl.BlockSpec((1, 1, ff, d), lambda b, l, be, st, nu: (l[0], be[b], 0, 0))],
            out_specs=pl.BlockSpec((MOE_BLOCK, d), lambda b, l, be, st, nu: (b, 0)),
            scratch_shapes=[pltpu.VMEM((MOE_BLOCK, d), F32), pltpu.VMEM((MOE_BLOCK, d), F32),
                            pltpu.VMEM((d, ff), BF16), pltpu.VMEM((d, ff), BF16),
                            pltpu.VMEM((ff, d), BF16),
                            pltpu.SemaphoreType.DMA((2,))]),
        out_shape=jax.ShapeDtypeStruct((n_blocks * MOE_BLOCK, d), F32),
        compiler_params=_cparams(("arbitrary",)),
        name="experts",
    )(larr, block_expert, slot_tok, block_rows, h2, w1, w3, w2)


def _combine_kernel(l_ref, dest_ref, y_hbm, x_ref, gates_ref, g2_ref, nw_ref, sh_ref, sc_ref,
                    o_ref, h_ref, ybuf, sem):
    i = pl.program_id(0)
    is_ctx = i == 0
    buf = i % 2

    def gather(tile, dst):
        def issue(r, carry):
            for k in range(2):
                slot = dest_ref[(tile * ROW_TILE + r) * 2 + k]
                pltpu.make_async_copy(y_hbm.at[pl.ds(slot, 1), :],
                                      ybuf.at[dst, pl.ds(k * ROW_TILE + r, 1), :],
                                      sem.at[dst]).start()
            return carry
        lax.fori_loop(0, ROW_TILE, issue, 0)

    @pl.when(i == 0)
    def _():
        gather(0, 0)

    pltpu.make_async_copy(y_hbm.at[pl.ds(0, 2 * ROW_TILE), :], ybuf.at[buf], sem.at[buf]).wait()

    @pl.when(i + 1 < pl.num_programs(0))
    def _():
        gather(i + 1, 1 - buf)

    gates = gates_ref[...]
    y = (gates[:, 0:1] * ybuf[buf, 0:ROW_TILE, :]
         + gates[:, 1:2] * ybuf[buf, ROW_TILE:2 * ROW_TILE, :])
    x = x_ref[...] + _mod_row(g2_ref, is_ctx) * y
    o_ref[...] = x
    h = x * lax.rsqrt(jnp.mean(x * x, axis=-1, keepdims=True) + NORM_EPS) * nw_ref[0]
    h = h * (1.0 + _mod_row(sc_ref, is_ctx)) + _mod_row(sh_ref, is_ctx)
    h_ref[...] = h.astype(h_ref.dtype)


def _combine(larr, dest, y_slots, x_new, gates, mods, norm1_w):
    t, d = x_new.shape
    depth = norm1_w.shape[0]
    nxt = lambda l: jnp.minimum(l[0] + 1, depth - 1)
    tile = pl.BlockSpec((ROW_TILE, d), lambda i, l, dd: (i, 0))
    return pl.pallas_call(
        _combine_kernel,
        grid_spec=pltpu.PrefetchScalarGridSpec(
            num_scalar_prefetch=2, grid=(t // ROW_TILE,),
            in_specs=[pl.BlockSpec(memory_space=pl.ANY),
                      tile,
                      pl.BlockSpec((ROW_TILE, HEAD_DIM), lambda i, l, dd: (i, 0)),
                      pl.BlockSpec((1, 8, d), lambda i, l, dd: (l[0], 0, 5)),
                      pl.BlockSpec((1, 1, d), lambda i, l, dd: (nxt(l), 0, 0)),
                      pl.BlockSpec((1, 8, d), lambda i, l, dd: (nxt(l), 0, 0)),
                      pl.BlockSpec((1, 8, d), lambda i, l, dd: (nxt(l), 0, 1))],
            out_specs=[tile, tile],
            scratch_shapes=[pltpu.VMEM((2, 2 * ROW_TILE, d), F32),
                            pltpu.SemaphoreType.DMA((2,))]),
        out_shape=[jax.ShapeDtypeStruct((t, d), F32), jax.ShapeDtypeStruct((t, d), BF16)],
        compiler_params=_cparams(("arbitrary",)),
        name="combine",
    )(larr, dest, y_slots, x_new, gates, mods, norm1_w.reshape(depth, 1, d), mods, mods)


def _dispatch(ids, sorted_tiles, tile_counts):
    t = ids.shape[0]
    n_tiles = t // ROW_TILE
    n_assign = 2 * t
    experts = ids[:, 0:2]
    cnt = tile_counts[:, 0, :N_EXPERTS]
    tile_end = jnp.cumsum(cnt, axis=0)
    tile_off = tile_end - cnt
    counts = tile_end[-1]
    padded = (counts + MOE_BLOCK - 1) // MOE_BLOCK * MOE_BLOCK
    padded_end = jnp.cumsum(padded)
    padded_start = padded_end - padded
    run_start = (padded_start[None, :] + tile_off).reshape(-1)
    tile_of_tok = jnp.arange(t, dtype=jnp.int32)[:, None] // ROW_TILE
    dest = (run_start[tile_of_tok * N_EXPERTS + experts] + ids[:, 2:4]).reshape(n_assign)
    n_blocks = -(-n_assign // MOE_BLOCK) + N_EXPERTS
    block_start = jnp.arange(n_blocks, dtype=jnp.int32) * MOE_BLOCK
    block_expert = jnp.clip(jnp.searchsorted(padded_end, block_start, side='right'),
                            0, N_EXPERTS - 1).astype(jnp.int32)
    filled_end = (padded_start + counts)[block_expert]
    block_rows = jnp.clip(filled_end - block_start, 0, MOE_BLOCK).astype(jnp.int32)
    rank = (block_start - padded_start[block_expert])[:, None] + jnp.arange(MOE_BLOCK, dtype=jnp.int32)
    ends = tile_end.T[block_expert]
    tile = jnp.sum((rank[:, :, None] >= ends[:, None, :]).astype(jnp.int32), axis=-1)
    tile = jnp.minimum(tile, n_tiles - 1)
    local_start = (jnp.cumsum(cnt, axis=1) - cnt).reshape(-1)
    key = tile * N_EXPERTS + block_expert[:, None]
    place = tile * (2 * ROW_TILE) + local_start[key] + rank - tile_off.reshape(-1)[key]
    place = jnp.clip(place, 0, n_tiles * 2 * ROW_TILE - 1)
    slot_tok = jnp.where(rank < counts[block_expert][:, None],
                         sorted_tiles[:, 0, :].reshape(-1)[place], 0)
    return dest.astype(jnp.int32), slot_tok.reshape(-1).astype(jnp.int32), block_expert, block_rows


def _diff_lane_fields():
    lane = np.arange(HEAD_DIM)
    part, m, half, f = lane // 64, (lane % 64) // 32, (lane % 32) // 16, lane % 16
    return part, m, half, f


def _rope_tables(ctx_len, seq):
    part, m, half, f = _diff_lane_fields()
    nf = HEAD_DIM // 8
    inv = ROPE_BASE ** (-jnp.arange(nf, dtype=F32) / nf)
    pos = jnp.arange(seq)
    prow = (pos // GRID_W).astype(F32)
    pcol = (pos % GRID_W).astype(F32)
    p = jnp.where(jnp.asarray(half)[None, :] == 0, prow[:, None], pcol[:, None])
    ang = p * inv[jnp.asarray(f)][None, :]
    sign = jnp.where(jnp.asarray(part) == 0, -1.0, 1.0).astype(F32)[None, :]
    cos = jnp.concatenate([jnp.ones((ctx_len, HEAD_DIM), F32), jnp.cos(ang)], axis=0)
    sin = jnp.concatenate([jnp.zeros((ctx_len, HEAD_DIM), F32), jnp.sin(ang) * sign], axis=0)
    return cos, sin


def _permute_diff_cols(w, n_heads):
    lead = w.shape[:-1]
    w = w.reshape(lead + (n_heads, 2, 2, 2, 16))
    nd = len(lead)
    w = jnp.transpose(w, tuple(range(nd)) + (nd, nd + 3, nd + 1, nd + 2, nd + 4))
    return w.reshape(lead + (n_heads * HEAD_DIM,))


def _qk_weight_table(diff_q_norm, diff_k_norm, na_q_norm, na_k_norm):
    part, m, half, f = _diff_lane_fields()
    src = half * 32 + part * 16 + f
    dqk = HEAD_DIM // 2
    dq = diff_q_norm[:, src] * (dqk ** -0.5 * LOG2E)
    dk = diff_k_norm[:, src]
    nq = na_q_norm * (HEAD_DIM ** -0.5 * LOG2E)
    rows = jnp.stack([dq, dk, nq, na_k_norm], axis=1)
    return jnp.concatenate([rows, jnp.zeros_like(rows)], axis=1)


def kernel(x, c, ctx, c_ctx, ada_w, ada_b, norm1_w, norm2_w, w_in, w_out, diff_q_norm, diff_k_norm,
           diff_lq1, diff_lk1, diff_lq2, diff_lk2, diff_subln, na_q_norm, na_k_norm, na_rpb,
           moe_w_group, moe_w_expert, moe_w1, moe_w3, moe_w2):
    _, seq, d = x.shape
    ctx_len = ctx.shape[1]
    depth = ada_w.shape[0]
    assert ctx_len == ROW_TILE and seq % ROW_TILE == 0 and x.shape[0] == 1
    n_heads = d // HEAD_DIM
    n_diff = n_heads // 2
    n_na = n_heads - n_diff
    seg = n_diff * HEAD_DIM
    t = ctx_len + seq
    n_tiles = t // ROW_TILE

    xa = jnp.concatenate([ctx[0], x[0]], axis=0)
    c2 = jnp.zeros((8, d), F32).at[0].set(c[0]).at[1].set(c_ctx)
    mods = _adaln(c2, ada_w, ada_b)

    w_in_b = w_in.astype(BF16)
    w_q = _permute_diff_cols(w_in_b[..., :seg], n_diff)
    w_k = _permute_diff_cols(w_in_b[..., seg:2 * seg], n_diff)
    w_out_b = w_out.astype(BF16)
    qkw = _qk_weight_table(diff_q_norm, diff_k_norm, na_q_norm, na_k_norm)
    cos_t, sin_t = _rope_tables(ctx_len, seq)
    na_slabs, na_slab_idx = _na_bias_slabs(na_rpb, n_tiles)

    lam_init = jnp.asarray([0.8 - 0.6 * math.exp(-0.3 * l) for l in range(depth)], F32)
    lam = (jnp.exp(jnp.sum(diff_lq1 * diff_lk1, axis=-1)) - jnp.exp(jnp.sum(diff_lq2 * diff_lk2, axis=-1))
           + lam_init)
    bound = (HEAD_DIM // 2) * jnp.max(jnp.abs(qkw[:, 0]), axis=-1) * jnp.max(jnp.abs(qkw[:, 1]), axis=-1)
    bound = bound * 1.02 + 0.01
    bound = jnp.where(bound <= MAX_STATIC_SHIFT, bound, -1.0)
    lam_tab = jnp.stack([lam, 1.0 - lam_init, bound], axis=1).reshape(DIFF_TAB * depth)

    wr = jnp.concatenate([moe_w_expert, moe_w_group,
                          jnp.zeros((depth, d, HEAD_DIM - N_EXPERTS - N_GROUPS), F32)], axis=-1)
    wr_hi = wr.astype(BF16)
    wr_lo = (wr - wr_hi.astype(F32)).astype(BF16)
    wr_split = jnp.stack([wr_hi, wr_lo], axis=1)

    h = _norm_mod(jnp.zeros((1,), jnp.int32), xa, norm1_w, mods)
    for l in range(depth):
        larr = jnp.full((1,), l, jnp.int32)
        proj = _in_proj(larr, h, w_q, w_k, w_in_b, qkw, cos_t, sin_t)
        d_out = _diff_attn(larr, lam_tab, proj, diff_subln, n_diff, ctx_len)
        n_out = _na_attn(larr, proj, na_slabs, na_slab_idx, n_na, n_diff)
        x_new, h2, ids, gates, sorted_tiles, tile_counts = _out_router(
            larr, d_out, n_out, w_out_b, xa, mods, norm2_w, wr_split)
        dest, slot_tok, block_expert, block_rows = _dispatch(ids, sorted_tiles, tile_counts)
        y_slots = _experts(larr, block_expert, slot_tok, block_rows, h2, moe_w1, moe_w3, moe_w2)
        xa, h = _combine(larr, dest, y_slots, x_new, gates, mods, norm1_w)
    return xa[ctx_len:][None]
```

```python
import functools
import math

import numpy as np
import jax
import jax.numpy as jnp
from jax import lax
from jax.experimental import pallas as pl
from jax.experimental.pallas import tpu as pltpu

F32 = jnp.float32
BF16 = jnp.bfloat16

GRID_W = 64
HEAD_DIM = 128
WIN_R = 8
WIN_C = 16
ROPE_BASE = 10000.0
N_GROUPS = 4
EXPERTS_PER_GROUP = 8
N_EXPERTS = N_GROUPS * EXPERTS_PER_GROUP
NORM_EPS = 1e-6
NEG_INF = -1e30
LOG2E = 1.4426950408889634

ROW_TILE = 256
NA_ROWS = ROW_TILE // GRID_W
NA_WIN_TILES = 3
MOE_BLOCK = 256
GATHER_CHUNK = 32
ADA_TN = 1536
IN_PROJ_TM = (384, 256)
DIFF_TK = 512
DIFF_TK_ALL = (1408, 768, 640, 512, 256)
VMEM_LIMIT = 56 * 1024 * 1024


def _cparams(sem, vmem=VMEM_LIMIT):
    return pltpu.CompilerParams(dimension_semantics=sem, vmem_limit_bytes=vmem)


def _nt_dot(a, b):
    return lax.dot_general(a, b, (((1,), (1,)), ((), ())), preferred_element_type=F32)


def _adaln_kernel(c_ref, w_ref, b_ref, o_ref):
    c = c_ref[...]
    a = c / (1.0 + jnp.exp(-c))
    o_ref[0] = jnp.dot(a.astype(BF16), w_ref[0].astype(BF16),
                       preferred_element_type=F32) + b_ref[0]


def _adaln(c2, ada_w, ada_b):
    depth, d, six_d = ada_w.shape
    tn = ADA_TN
    return pl.pallas_call(
        _adaln_kernel,
        grid=(depth, six_d // tn),
        in_specs=[pl.BlockSpec((8, d), lambda l, j: (0, 0)),
                  pl.BlockSpec((1, d, tn), lambda l, j: (l, 0, j)),
                  pl.BlockSpec((1, 1, tn), lambda l, j: (l, 0, j))],
        out_specs=pl.BlockSpec((1, 8, tn), lambda l, j: (l, 0, j)),
        out_shape=jax.ShapeDtypeStruct((depth, 8, six_d), F32),
        compiler_params=_cparams(("arbitrary", "arbitrary")),
        name="adaln",
    )(c2, ada_w, ada_b.reshape(depth, 1, six_d))


def _mod_spec(d, chunk):
    return pl.BlockSpec((1, 8, d), lambda i, l: (l[0], 0, chunk))


def _mod_row(ref, is_ctx):
    return jnp.where(is_ctx, ref[0, 1:2, :], ref[0, 0:1, :])


def _norm_mod_kernel(l_ref, x_ref, nw_ref, sh_ref, sc_ref, o_ref):
    is_ctx = pl.program_id(0) == 0
    x = x_ref[...]
    y = x * lax.rsqrt(jnp.mean(x * x, axis=-1, keepdims=True) + NORM_EPS) * nw_ref[0]
    y = y * (1.0 + _mod_row(sc_ref, is_ctx)) + _mod_row(sh_ref, is_ctx)
    o_ref[...] = y.astype(o_ref.dtype)


def _norm_mod(larr, xa, norm_w, mods):
    t, d = xa.shape
    depth = norm_w.shape[0]
    return pl.pallas_call(
        _norm_mod_kernel,
        grid_spec=pltpu.PrefetchScalarGridSpec(
            num_scalar_prefetch=1, grid=(t // ROW_TILE,),
            in_specs=[pl.BlockSpec((ROW_TILE, d), lambda i, l: (i, 0)),
                      pl.BlockSpec((1, 1, d), lambda i, l: (l[0], 0, 0)),
                      _mod_spec(d, 0), _mod_spec(d, 1)],
            out_specs=pl.BlockSpec((ROW_TILE, d), lambda i, l: (i, 0))),
        out_shape=jax.ShapeDtypeStruct((t, d), BF16),
        compiler_params=_cparams(("arbitrary",)),
        name="norm_mod",
    )(larr, xa, norm_w.reshape(depth, 1, d), mods, mods)


def _in_proj_kernel(l_ref, x_ref, wq_ref, wk_ref, wm_ref, wh_ref, qkw_ref, cos_ref, sin_ref, o_ref,
                    acc_ref, *, seg):
    n_chunks = seg // HEAD_DIM
    x = x_ref[...]
    a = lax.broadcasted_iota(jnp.int32, (HEAD_DIM, HEAD_DIM), 0)
    b = lax.broadcasted_iota(jnp.int32, (HEAD_DIM, HEAD_DIM), 1)
    map_mat = ((a & 32) == (b & 32)).astype(BF16)
    head_mat = jnp.ones((HEAD_DIM, HEAD_DIM), BF16)
    weights = (wq_ref[0], wk_ref[0], wm_ref[0, :, 0:seg], wm_ref[0, :, seg:2 * seg],
               wh_ref[0, :, 0:seg], wh_ref[0, :, seg:2 * seg])
    norm_row = {0: 0, 1: 1, 3: 2, 4: 3}

    def normed(buf, c, group_mat, inv_n, wrow):
        xc = acc_ref[buf, :, c * HEAD_DIM:(c + 1) * HEAD_DIM]
        ss = jnp.dot((xc * xc).astype(BF16), group_mat, preferred_element_type=F32)
        return xc * lax.rsqrt(ss * inv_n + NORM_EPS) * wrow

    for j in range(6):
        buf = j % 2
        acc_ref[buf] = jnp.dot(x, weights[j], preferred_element_type=F32)
        for c in range(n_chunks):
            cols = slice(j * seg + c * HEAD_DIM, j * seg + (c + 1) * HEAD_DIM)
            if j in (2, 5):
                y = acc_ref[buf, :, c * HEAD_DIM:(c + 1) * HEAD_DIM]
            elif j < 2:
                wrow = qkw_ref[0, norm_row[j]:norm_row[j] + 1, :]
                y = normed(buf, c, map_mat, 2.0 / HEAD_DIM, wrow)
                y = y * cos_ref[...] + pltpu.roll(y, HEAD_DIM // 2, 1) * sin_ref[...]
            else:
                wrow = qkw_ref[0, norm_row[j]:norm_row[j] + 1, :]
                y = normed(buf, c, head_mat, 1.0 / HEAD_DIM, wrow)
            o_ref[:, cols] = y.astype(o_ref.dtype)


def _pick_tile(n, candidates):
    for c in candidates:
        if n % c == 0:
            return c
    raise ValueError(f"no tile for {n}")


def _in_proj(larr, h, w_q, w_k, w_in_b, qkw, cos_t, sin_t):
    t, d = h.shape
    seg = w_q.shape[2]
    tm = _pick_tile(t, IN_PROJ_TM)
    once = pl.Buffered(1)

    def weight_spec(width, col):
        return pl.BlockSpec((1, d, width), lambda i, l: (l[0], 0, col), pipeline_mode=once)

    return pl.pallas_call(
        functools.partial(_in_proj_kernel, seg=seg),
        grid_spec=pltpu.PrefetchScalarGridSpec(
            num_scalar_prefetch=1, grid=(t // tm,),
            in_specs=[pl.BlockSpec((tm, d), lambda i, l: (i, 0)),
                      weight_spec(seg, 0), weight_spec(seg, 0),
                      weight_spec(2 * seg, 1), weight_spec(2 * seg, 2),
                      pl.BlockSpec((1, 8, HEAD_DIM), lambda i, l: (l[0], 0, 0)),
                      pl.BlockSpec((tm, HEAD_DIM), lambda i, l: (i, 0)),
                      pl.BlockSpec((tm, HEAD_DIM), lambda i, l: (i, 0))],
            out_specs=pl.BlockSpec((tm, 6 * seg), lambda i, l: (i, 0)),
            scratch_shapes=[pltpu.VMEM((2, tm, seg), F32)]),
        out_shape=jax.ShapeDtypeStruct((t, 6 * seg), BF16),
        compiler_params=_cparams(("arbitrary",)),
        name="in_proj",
    )(larr, h, w_q, w_k, w_in_b, w_in_b, qkw, cos_t, sin_t)


DIFF_HEADS_PER_STEP = (4, 2, 1)
DIFF_TAB = 3
MAX_STATIC_SHIFT = 60.0


def _diff_attn_kernel(l_ref, lam_ref, q_ref, k_ref, v_ref, sw_ref, o_ref, m_ref, s_ref, part_ref,
                      acc_ref, *, ctx_len, tk, tk_all, heads):
    i = pl.program_id(1)
    layer = l_ref[0]
    tq = q_ref.shape[0]
    n_lat = (k_ref.shape[0] - ctx_len) // tk
    lam = lam_ref[DIFF_TAB * layer]
    out_scale = lam_ref[DIFF_TAB * layer + 1]
    bound = lam_ref[DIFF_TAB * layer + 2]
    lane = lax.broadcasted_iota(jnp.int32, (1, HEAD_DIM), 1)
    in_map0 = (lane & 32) == 0

    def head_cols(hh):
        return slice(hh * HEAD_DIM, (hh + 1) * HEAD_DIM)

    def masked_q(hh):
        q = q_ref[:, head_cols(hh)]
        return (jnp.where(in_map0, q, jnp.zeros_like(q)), jnp.where(in_map0, jnp.zeros_like(q), q))

    def finish(hh, l0, l1):
        o = acc_ref[hh, 0] / l0 - lam * (acc_ref[hh, 1] / l1)
        o = o * lax.rsqrt(jnp.mean(o * o, axis=-1, keepdims=True) + NORM_EPS) * sw_ref[0] * out_scale
        o_ref[:, head_cols(hh)] = o.astype(o_ref.dtype)

    @pl.when(bound >= 0.0)
    def _():
        def attend(hh, n_keys, chunk):
            hs = head_cols(hh)
            q_both = jnp.concatenate(masked_q(hh), axis=0)
            for lo in range(0, n_keys, chunk):
                p = jnp.exp2(_nt_dot(q_both, k_ref[lo:lo + chunk, hs]) - bound)
                prt = p[:, 0:HEAD_DIM]
                for j in range(1, chunk // HEAD_DIM):
                    prt = prt + p[:, j * HEAD_DIM:(j + 1) * HEAD_DIM]
                pv = jnp.dot(p.astype(BF16), v_ref[lo:lo + chunk, hs], preferred_element_type=F32)
                for mi in range(2):
                    if lo == 0:
                        part_ref[hh, mi] = prt[mi * tq:(mi + 1) * tq]
                        acc_ref[hh, mi] = pv[mi * tq:(mi + 1) * tq]
                    else:
                        part_ref[hh, mi] += prt[mi * tq:(mi + 1) * tq]
                        acc_ref[hh, mi] += pv[mi * tq:(mi + 1) * tq]

        def run(n_keys, chunk):
            for hh in range(heads):
                attend(hh, n_keys, chunk)
                finish(hh, jnp.sum(part_ref[hh, 0], axis=-1, keepdims=True),
                       jnp.sum(part_ref[hh, 1], axis=-1, keepdims=True))

        @pl.when(i == 0)
        def _():
            run(ctx_len, ctx_len)

        @pl.when(i > 0)
        def _():
            run(k_ref.shape[0], tk_all)

    @pl.when(bound < 0.0)
    def _():
        for hh in range(heads):
            hs = head_cols(hh)
            qs = masked_q(hh)
            m_ref[...] = jnp.full(m_ref.shape, -jnp.inf, F32)
            s_ref[...] = jnp.zeros(s_ref.shape, F32)
            acc_ref[hh] = jnp.zeros(acc_ref.shape[1:], F32)

            def step(kc, vc, hh=hh, qs=qs):
                for mi in range(2):
                    s = _nt_dot(qs[mi], kc)
                    m_prev = m_ref[mi]
                    m_new = jnp.maximum(m_prev, jnp.max(s, axis=-1, keepdims=True))
                    alpha = jnp.exp2(m_prev - m_new)
                    p = jnp.exp2(s - m_new)
                    s_ref[mi] = alpha * s_ref[mi] + jnp.sum(p, axis=-1, keepdims=True)
                    acc_ref[hh, mi] = alpha * acc_ref[hh, mi] + jnp.dot(
                        p.astype(BF16), vc, preferred_element_type=F32)
                    m_ref[mi] = m_new

            step(k_ref[0:ctx_len, hs], v_ref[0:ctx_len, hs])

            def body(c, carry, step=step, hs=hs):
                off = pl.multiple_of(ctx_len + c * tk, math.gcd(ctx_len, tk))
                step(k_ref[pl.ds(off, tk), hs], v_ref[pl.ds(off, tk), hs])
                return carry
            lax.fori_loop(0, jnp.where(i == 0, 0, n_lat), body, 0)
            finish(hh, s_ref[0], s_ref[1])


def _diff_attn(larr, lam_tab, proj, subln, n_heads, ctx_len):
    t = proj.shape[0]
    depth = subln.shape[0]
    tq = ROW_TILE
    tk = _pick_tile(t - ctx_len, (DIFF_TK, 256))
    heads = _pick_tile(n_heads, DIFF_HEADS_PER_STEP)
    n_hb = n_heads // heads
    wb = heads * HEAD_DIM
    return pl.pallas_call(
        functools.partial(_diff_attn_kernel, ctx_len=ctx_len, tk=tk,
                          tk_all=_pick_tile(t, DIFF_TK_ALL), heads=heads),
        grid_spec=pltpu.PrefetchScalarGridSpec(
            num_scalar_prefetch=1, grid=(n_hb, t // tq),
            in_specs=[pl.BlockSpec(memory_space=pltpu.SMEM),
                      pl.BlockSpec((tq, wb), lambda h, i, l: (i, h)),
                      pl.BlockSpec((t, wb), lambda h, i, l: (0, n_hb + h)),
                      pl.BlockSpec((t, wb), lambda h, i, l: (0, 2 * n_hb + h)),
                      pl.BlockSpec((1, 1, HEAD_DIM), lambda h, i, l: (l[0], 0, 0))],
            out_specs=pl.BlockSpec((tq, wb), lambda h, i, l: (i, h)),
            scratch_shapes=[pltpu.VMEM((2, tq, 1), F32), pltpu.VMEM((2, tq, 1), F32),
                            pltpu.VMEM((heads, 2, tq, HEAD_DIM), F32),
                            pltpu.VMEM((heads, 2, tq, HEAD_DIM), F32)]),
        out_shape=jax.ShapeDtypeStruct((t, n_heads * HEAD_DIM), BF16),
        compiler_params=_cparams(("arbitrary", "arbitrary")),
        name="diff_attn",
    )(larr, lam_tab, proj, proj, proj, subln.reshape(depth, 1, HEAD_DIM))


def _na_kernel(l_ref, idx_ref, q_ref, kc_ref, k0_ref, k1_ref, k2_ref, vc_ref, v0_ref, v1_ref, v2_ref,
               slab_ref, o_ref, *, n_heads):
    kw = (k0_ref, k1_ref, k2_ref)
    vw = (v0_ref, v1_ref, v2_ref)
    g = pl.program_id(0)
    n_tiles = pl.num_programs(0)
    pat = jnp.where(g == 0, 0, jnp.where(g == 1, 1, jnp.where(g == n_tiles - 1, 3, 2)))
    n_kr = NA_WIN_TILES * NA_ROWS
    low_lanes = lax.broadcasted_iota(jnp.int32, (GRID_W, 2 * GRID_W), 1) < GRID_W

    def bias_tile(h, j):
        rows = []
        for a in range(NA_ROWS):
            pairs = []
            for b in range(NA_ROWS * j, NA_ROWS * (j + 1), 2):
                i0 = idx_ref[(pat * NA_ROWS + a) * n_kr + b]
                i1 = idx_ref[(pat * NA_ROWS + a) * n_kr + b + 1]
                pairs.append(jnp.where(low_lanes, slab_ref[0, h, i0], slab_ref[0, h, i1]))
            rows.append(jnp.concatenate(pairs, axis=1))
        return jnp.concatenate(rows, axis=0)

    for h in range(n_heads):
        hs = slice(h * HEAD_DIM, (h + 1) * HEAD_DIM)
        qh = q_ref[:, hs]
        s_c = _nt_dot(qh, kc_ref[:, hs])
        s_w = [_nt_dot(qh, kw[j][:, hs]) + bias_tile(h, j) for j in range(NA_WIN_TILES)]
        m = jnp.max(s_c, axis=-1, keepdims=True)
        for s in s_w:
            m = jnp.maximum(m, jnp.max(s, axis=-1, keepdims=True))
        p_c = jnp.exp2(s_c - m)
        denom = jnp.sum(p_c, axis=-1, keepdims=True)
        o = jnp.dot(p_c.astype(BF16), vc_ref[:, hs], preferred_element_type=F32)
        for j in range(NA_WIN_TILES):
            p = jnp.exp2(s_w[j] - m)
            denom = denom + jnp.sum(p, axis=-1, keepdims=True)
            o = o + jnp.dot(p.astype(BF16), vw[j][:, hs], preferred_element_type=F32)
        o_ref[:, hs] = (o / denom).astype(o_ref.dtype)


def _na_win_base(g, n_tiles):
    return 1 + jnp.clip(g - 2, 0, n_tiles - 1 - NA_WIN_TILES)


def _na_attn(larr, proj, slabs, slab_idx, n_heads, n_diff_heads):
    t = proj.shape[0]
    n_tiles = t // ROW_TILE
    w = n_heads * HEAD_DIM
    qcol = 3 * n_diff_heads * HEAD_DIM // w
    blk = (ROW_TILE, w)

    def win_spec(col, j):
        return pl.BlockSpec(blk, lambda g, l, ix: (_na_win_base(g, n_tiles) + j, col))

    in_specs = [pl.BlockSpec(blk, lambda g, l, ix: (g, qcol)),
                pl.BlockSpec(blk, lambda g, l, ix: (0, qcol + 1))]
    in_specs += [win_spec(qcol + 1, j) for j in range(NA_WIN_TILES)]
    in_specs += [pl.BlockSpec(blk, lambda g, l, ix: (0, qcol + 2))]
    in_specs += [win_spec(qcol + 2, j) for j in range(NA_WIN_TILES)]
    in_specs += [pl.BlockSpec((1,) + slabs.shape[1:], lambda g, l, ix: (l[0], 0, 0, 0, 0))]
    return pl.pallas_call(
        functools.partial(_na_kernel, n_heads=n_heads),
        grid_spec=pltpu.PrefetchScalarGridSpec(
            num_scalar_prefetch=2, grid=(n_tiles,),
            in_specs=in_specs,
            out_specs=pl.BlockSpec(blk, lambda g, l, ix: (g, 0))),
        out_shape=jax.ShapeDtypeStruct((t, w), BF16),
        compiler_params=_cparams(("arbitrary",)),
        name="na_attn",
    )(larr, slab_idx, *([proj] * 9), slabs)


def _na_bias_slabs(rpb, n_tiles):
    depth, n_h = rpb.shape[:2]
    rows = (n_tiles - 1) * NA_ROWS
    kr = min(WIN_R, rows)
    n_dr = 2 * WIN_R - 1
    n_kr = NA_WIN_TILES * NA_ROWS
    qc = np.arange(GRID_W)[:, None]
    kc = np.arange(GRID_W)[None, :]
    cs = np.clip(qc - WIN_C // 2, 0, GRID_W - WIN_C)
    col_valid = (kc >= cs) & (kc < cs + WIN_C)
    pad = GRID_W - WIN_C
    row = jnp.concatenate([jnp.repeat(rpb[..., :1], pad, axis=-1), rpb,
                           jnp.repeat(rpb[..., -1:], pad, axis=-1)], axis=-1)
    slabs = jnp.stack([row[..., GRID_W - 1 - c:2 * GRID_W - 1 - c] for c in range(GRID_W)], axis=-2)
    slabs = jnp.where(col_valid[None, None, None], slabs, NEG_INF) * LOG2E
    masked = jnp.full((depth, n_h, 1, GRID_W, GRID_W), NEG_INF * LOG2E, F32)
    slabs = jnp.concatenate([slabs, masked], axis=2)
    slab_idx = np.full((4, NA_ROWS, n_kr), n_dr, np.int32)
    for p, gl in enumerate((0, 1, n_tiles - 2)):
        qr = NA_ROWS * gl + np.arange(NA_ROWS)[:, None]
        win_start = NA_ROWS * int(np.clip(gl - 1, 0, n_tiles - 1 - NA_WIN_TILES))
        krow = win_start + np.arange(n_kr)[None, :]
        rs = np.clip(qr - kr // 2, 0, rows - kr)
        row_valid = (krow >= rs) & (krow < rs + kr)
        slab_idx[p + 1] = np.where(row_valid, krow - qr + (WIN_R - 1), n_dr)
    slabs = jnp.concatenate([slabs, slabs], axis=-1)
    return slabs, jnp.asarray(slab_idx.reshape(-1))


def _out_router_kernel(l_ref, d_ref, n_ref, wa_ref, wb_ref, x_ref, g1_ref, nw_ref, sh_ref, sc_ref,
                       wr_ref, xo_ref, h_ref, ids_ref, gates_ref):
    is_ctx = pl.program_id(0) == 0
    acc = jnp.dot(d_ref[...], wa_ref[0], preferred_element_type=F32)
    acc = acc + jnp.dot(n_ref[...], wb_ref[0], preferred_element_type=F32)
    x = x_ref[...] + _mod_row(g1_ref, is_ctx) * acc
    xo_ref[...] = x
    y = x * lax.rsqrt(jnp.mean(x * x, axis=-1, keepdims=True) + NORM_EPS) * nw_ref[0]
    y = y * (1.0 + _mod_row(sc_ref, is_ctx)) + _mod_row(sh_ref, is_ctx)
    h_ref[...] = y

    y_hi = y.astype(BF16)
    y_lo = (y - y_hi.astype(F32)).astype(BF16)
    lg = jnp.dot(y_hi, wr_ref[0, 0], preferred_element_type=F32)
    lg = lg + (jnp.dot(y_lo, wr_ref[0, 0], preferred_element_type=F32)
               + jnp.dot(y_hi, wr_ref[0, 1], preferred_element_type=F32))

    lane = lax.broadcasted_iota(jnp.int32, lg.shape, 1).astype(F32)
    big = jnp.float32(1e9)
    is_g = (lane >= N_EXPERTS) & (lane < N_EXPERTS + N_GROUPS)
    gl = jnp.where(is_g, lg, -jnp.inf)
    gmax = jnp.max(gl, axis=-1, keepdims=True)
    gsel = jnp.min(jnp.where(gl == gmax, lane, big), axis=-1, keepdims=True) - N_EXPERTS
    g_gate = 1.0 / jnp.sum(jnp.where(is_g, jnp.exp(gl - gmax), 0.0), axis=-1, keepdims=True)
    lo = gsel * EXPERTS_PER_GROUP
    el = jnp.where((lane >= lo) & (lane < lo + EXPERTS_PER_GROUP), lg, -jnp.inf)
    e1 = jnp.max(el, axis=-1, keepdims=True)
    i1 = jnp.min(jnp.where(el == e1, lane, big), axis=-1, keepdims=True)
    el2 = jnp.where(lane == i1, -jnp.inf, el)
    e2 = jnp.max(el2, axis=-1, keepdims=True)
    i2 = jnp.min(jnp.where(el2 == e2, lane, big), axis=-1, keepdims=True)
    r = jnp.exp(e2 - e1)
    w1 = 1.0 / (1.0 + r)
    w2 = r / (1.0 + r)
    ids_ref[...] = jnp.where(lane == 0, i1, jnp.where(lane == 1, i2, 0.0)).astype(jnp.int32)
    gates_ref[...] = jnp.where(lane == 0, g_gate * w1, jnp.where(lane == 1, g_gate * w2, 0.0))


def _out_router(larr, d_out, n_out, w_out_b, xa, mods, norm2_w, wr):
    t, d = xa.shape
    half = d_out.shape[1]
    depth = norm2_w.shape[0]
    tile = lambda width: pl.BlockSpec((ROW_TILE, width), lambda i, l: (i, 0))
    return pl.pallas_call(
        _out_router_kernel,
        grid_spec=pltpu.PrefetchScalarGridSpec(
            num_scalar_prefetch=1, grid=(t // ROW_TILE,),
            in_specs=[tile(half), tile(half),
                      pl.BlockSpec((1, half, d), lambda i, l: (l[0], 0, 0)),
                      pl.BlockSpec((1, half, d), lambda i, l: (l[0], 1, 0)),
                      tile(d), _mod_spec(d, 2),
                      pl.BlockSpec((1, 1, d), lambda i, l: (l[0], 0, 0)),
                      _mod_spec(d, 3), _mod_spec(d, 4),
                      pl.BlockSpec((1, 2, d, HEAD_DIM), lambda i, l: (l[0], 0, 0, 0))],
            out_specs=[tile(d), tile(d), tile(HEAD_DIM), tile(HEAD_DIM)]),
        out_shape=[jax.ShapeDtypeStruct((t, d), F32), jax.ShapeDtypeStruct((t, d), F32),
                   jax.ShapeDtypeStruct((t, HEAD_DIM), jnp.int32),
                   jax.ShapeDtypeStruct((t, HEAD_DIM), F32)],
        compiler_params=_cparams(("arbitrary",)),
        name="out_router",
    )(larr, d_out, n_out, w_out_b, w_out_b, xa, mods, norm2_w.reshape(depth, 1, d), mods, mods, wr)


def _expert_kernel(l_ref, be_ref, st_ref, nu_ref, h_hbm, w1_ref, w3_ref, w2_ref, y_ref,
                   xbuf0, xbuf1, wb1, wb3, wb2, sem):
    b = pl.program_id(0)
    n_blocks = pl.num_programs(0)
    xbufs = (xbuf0, xbuf1)
    n_chunks = MOE_BLOCK // GATHER_CHUNK

    def rows_of(blk):
        return jnp.where(blk < n_blocks, nu_ref[jnp.minimum(blk, n_blocks - 1)], 0)

    def gather(blk, slot):
        n_rows = rows_of(blk)
        for c in range(n_chunks):
            @pl.when(c * GATHER_CHUNK < n_rows)
            def _():
                for r in range(c * GATHER_CHUNK, (c + 1) * GATHER_CHUNK):
                    tok = st_ref[blk * MOE_BLOCK + r]
                    pltpu.make_async_copy(h_hbm.at[pl.ds(tok, 1), :],
                                          xbufs[slot].at[pl.ds(r, 1), :], sem.at[slot]).start()

    def wait_rows(blk, slot):
        n_rows = rows_of(blk)
        for c in range(n_chunks):
            @pl.when(c * GATHER_CHUNK < n_rows)
            def _():
                pltpu.make_async_copy(h_hbm.at[pl.ds(0, GATHER_CHUNK), :],
                                      xbufs[slot].at[pl.ds(c * GATHER_CHUNK, GATHER_CHUNK), :],
                                      sem.at[slot]).wait()

    @pl.when(b == 0)
    def _():
        xbuf0[...] = jnp.zeros(xbuf0.shape, F32)
        xbuf1[...] = jnp.zeros(xbuf1.shape, F32)
        gather(0, 0)

    used = rows_of(b) > 0
    new_expert = (b == 0) | (be_ref[b] != be_ref[jnp.maximum(b - 1, 0)])

    def run(slot, convert):
        wait_rows(b, slot)
        gather(b + 1, 1 - slot)
        xb = xbufs[slot][...].astype(BF16)
        if convert:
            wb1[...] = w1_ref[0, 0].astype(BF16)
        a = jnp.dot(xb, wb1[...], preferred_element_type=F32)
        if convert:
            wb3[...] = w3_ref[0, 0].astype(BF16)
        g = jnp.dot(xb, wb3[...], preferred_element_type=F32)
        if convert:
            wb2[...] = w2_ref[0, 0].astype(BF16)
        mid = (a / (1.0 + jnp.exp(-a))) * g
        y_ref[...] = jnp.dot(mid.astype(BF16), wb2[...], preferred_element_type=F32)

    for slot in range(2):
        for convert in (False, True):
            cond = used & (b % 2 == slot) & (new_expert if convert else jnp.logical_not(new_expert))
            pl.when(cond)(functools.partial(run, slot, convert))

    @pl.when(jnp.logical_not(used))
    def _():
        y_ref[...] = jnp.zeros(y_ref.shape, y_ref.dtype)


def _experts(larr, block_expert, slot_tok, block_rows, h2, w1, w3, w2):
    t, d = h2.shape
    ff = w1.shape[-1]
    n_blocks = block_expert.shape[0]
    return pl.pallas_call(
        _expert_kernel,
        grid_spec=pltpu.PrefetchScalarGridSpec(
            num_scalar_prefetch=4, grid=(n_blocks,),
            in_specs=[pl.BlockSpec(memory_space=pl.ANY),
                      pl.BlockSpec((1, 1, d, ff), lambda b, l, be, st, nu: (l[0], be[b], 0, 0)),
                      pl.BlockSpec((1, 1, d, ff), lambda b, l, be, st, nu: (l[0], be[b], 0, 0)),
                      pl.BlockSpec((1, 1, ff, d), lambda b, l, be, st, nu: (l[0], be[b], 0, 0))],
            out_specs=pl.BlockSpec((MOE_BLOCK, d), lambda b, l, be, st, nu: (b, 0)),
            scratch_shapes=[pltpu.VMEM((MOE_BLOCK, d), F32), pltpu.VMEM((MOE_BLOCK, d), F32),
                            pltpu.VMEM((d, ff), BF16), pltpu.VMEM((d, ff), BF16),
                            pltpu.VMEM((ff, d), BF16),
                            pltpu.SemaphoreType.DMA((2,))]),
        out_shape=jax.ShapeDtypeStruct((n_blocks * MOE_BLOCK, d), F32),
        compiler_params=_cparams(("arbitrary",)),
        name="experts",
    )(larr, block_expert, slot_tok, block_rows, h2, w1, w3, w2)


def _combine_kernel(l_ref, dest_ref, y_hbm, x_ref, gates_ref, g2_ref, nw_ref, sh_ref, sc_ref,
                    o_ref, h_ref, ybuf, sem):
    i = pl.program_id(0)
    is_ctx = i == 0
    buf = i % 2

    def gather(tile, dst):
        def issue(r, carry):
            for k in range(2):
                slot = dest_ref[(tile * ROW_TILE + r) * 2 + k]
                pltpu.make_async_copy(y_hbm.at[pl.ds(slot, 1), :],
                                      ybuf.at[dst, pl.ds(k * ROW_TILE + r, 1), :],
                                      sem.at[dst]).start()
            return carry
        lax.fori_loop(0, ROW_TILE, issue, 0)

    @pl.when(i == 0)
    def _():
        gather(0, 0)

    pltpu.make_async_copy(y_hbm.at[pl.ds(0, 2 * ROW_TILE), :], ybuf.at[buf], sem.at[buf]).wait()

    @pl.when(i + 1 < pl.num_programs(0))
    def _():
        gather(i + 1, 1 - buf)

    gates = gates_ref[...]
    y = (gates[:, 0:1] * ybuf[buf, 0:ROW_TILE, :]
         + gates[:, 1:2] * ybuf[buf, ROW_TILE:2 * ROW_TILE, :])
    x = x_ref[...] + _mod_row(g2_ref, is_ctx) * y
    o_ref[...] = x
    h = x * lax.rsqrt(jnp.mean(x * x, axis=-1, keepdims=True) + NORM_EPS) * nw_ref[0]
    h = h * (1.0 + _mod_row(sc_ref, is_ctx)) + _mod_row(sh_ref, is_ctx)
    h_ref[...] = h.astype(h_ref.dtype)


def _combine(larr, dest, y_slots, x_new, gates, mods, norm1_w):
    t, d = x_new.shape
    depth = norm1_w.shape[0]
    nxt = lambda l: jnp.minimum(l[0] + 1, depth - 1)
    tile = pl.BlockSpec((ROW_TILE, d), lambda i, l, dd: (i, 0))
    return pl.pallas_call(
        _combine_kernel,
        grid_spec=pltpu.PrefetchScalarGridSpec(
            num_scalar_prefetch=2, grid=(t // ROW_TILE,),
            in_specs=[pl.BlockSpec(memory_space=pl.ANY),
                      tile,
                      pl.BlockSpec((ROW_TILE, HEAD_DIM), lambda i, l, dd: (i, 0)),
                      pl.BlockSpec((1, 8, d), lambda i, l, dd: (l[0], 0, 5)),
                      pl.BlockSpec((1, 1, d), lambda i, l, dd: (nxt(l), 0, 0)),
                      pl.BlockSpec((1, 8, d), lambda i, l, dd: (nxt(l), 0, 0)),
                      pl.BlockSpec((1, 8, d), lambda i, l, dd: (nxt(l), 0, 1))],
            out_specs=[tile, tile],
            scratch_shapes=[pltpu.VMEM((2, 2 * ROW_TILE, d), F32),
                            pltpu.SemaphoreType.DMA((2,))]),
        out_shape=[jax.ShapeDtypeStruct((t, d), F32), jax.ShapeDtypeStruct((t, d), BF16)],
        compiler_params=_cparams(("arbitrary",)),
        name="combine",
    )(larr, dest, y_slots, x_new, gates, mods, norm1_w.reshape(depth, 1, d), mods, mods)


def _dispatch(ids):
    t = ids.shape[0]
    n_assign = 2 * t
    e_flat = ids.reshape(n_assign)
    onehot = (e_flat[:, None] == jnp.arange(N_EXPERTS, dtype=jnp.int32)[None, :]).astype(jnp.int32)
    csum = jnp.cumsum(onehot, axis=0)
    rank = jnp.sum(onehot * (csum - 1), axis=1)
    counts = csum[-1]
    padded = (counts + MOE_BLOCK - 1) // MOE_BLOCK * MOE_BLOCK
    padded_end = jnp.cumsum(padded)
    padded_start = padded_end - padded
    dest = (padded_start[e_flat] + rank).astype(jnp.int32)
    n_blocks = -(-n_assign // MOE_BLOCK) + N_EXPERTS
    tok = jnp.arange(n_assign, dtype=jnp.int32) // 2
    slot_tok = jnp.zeros((n_blocks * MOE_BLOCK,), jnp.int32).at[dest].set(tok)
    block_start = jnp.arange(n_blocks, dtype=jnp.int32) * MOE_BLOCK
    block_expert = jnp.clip(jnp.searchsorted(padded_end, block_start, side='right'),
                            0, N_EXPERTS - 1).astype(jnp.int32)
    filled_end = (padded_start + counts)[block_expert]
    block_rows = jnp.clip(filled_end - block_start, 0, MOE_BLOCK).astype(jnp.int32)
    return dest, slot_tok, block_expert, block_rows


def _diff_lane_fields():
    lane = np.arange(HEAD_DIM)
    part, m, half, f = lane // 64, (lane % 64) // 32, (lane % 32) // 16, lane % 16
    return part, m, half, f


def _rope_tables(ctx_len, seq):
    part, m, half, f = _diff_lane_fields()
    nf = HEAD_DIM // 8
    inv = ROPE_BASE ** (-jnp.arange(nf, dtype=F32) / nf)
    pos = jnp.arange(seq)
    prow = (pos // GRID_W).astype(F32)
    pcol = (pos % GRID_W).astype(F32)
    p = jnp.where(jnp.asarray(half)[None, :] == 0, prow[:, None], pcol[:, None])
    ang = p * inv[jnp.asarray(f)][None, :]
    sign = jnp.where(jnp.asarray(part) == 0, -1.0, 1.0).astype(F32)[None, :]
    cos = jnp.concatenate([jnp.ones((ctx_len, HEAD_DIM), F32), jnp.cos(ang)], axis=0)
    sin = jnp.concatenate([jnp.zeros((ctx_len, HEAD_DIM), F32), jnp.sin(ang) * sign], axis=0)
    return cos, sin


def _permute_diff_cols(w, n_heads):
    lead = w.shape[:-1]
    w = w.reshape(lead + (n_heads, 2, 2, 2, 16))
    nd = len(lead)
    w = jnp.transpose(w, tuple(range(nd)) + (nd, nd + 3, nd + 1, nd + 2, nd + 4))
    return w.reshape(lead + (n_heads * HEAD_DIM,))


def _qk_weight_table(diff_q_norm, diff_k_norm, na_q_norm, na_k_norm):
    part, m, half, f = _diff_lane_fields()
    src = half * 32 + part * 16 + f
    dqk = HEAD_DIM // 2
    dq = diff_q_norm[:, src] * (dqk ** -0.5 * LOG2E)
    dk = diff_k_norm[:, src]
    nq = na_q_norm * (HEAD_DIM ** -0.5 * LOG2E)
    rows = jnp.stack([dq, dk, nq, na_k_norm], axis=1)
    return jnp.concatenate([rows, jnp.zeros_like(rows)], axis=1)


def kernel(x, c, ctx, c_ctx, ada_w, ada_b, norm1_w, norm2_w, w_in, w_out, diff_q_norm, diff_k_norm,
           diff_lq1, diff_lk1, diff_lq2, diff_lk2, diff_subln, na_q_norm, na_k_norm, na_rpb,
           moe_w_group, moe_w_expert, moe_w1, moe_w3, moe_w2):
    _, seq, d = x.shape
    ctx_len = ctx.shape[1]
    depth = ada_w.shape[0]
    assert ctx_len == ROW_TILE and seq % ROW_TILE == 0 and x.shape[0] == 1
    n_heads = d // HEAD_DIM
    n_diff = n_heads // 2
    n_na = n_heads - n_diff
    seg = n_diff * HEAD_DIM
    t = ctx_len + seq
    n_tiles = t // ROW_TILE

    xa = jnp.concatenate([ctx[0], x[0]], axis=0)
    c2 = jnp.zeros((8, d), F32).at[0].set(c[0]).at[1].set(c_ctx)
    mods = _adaln(c2, ada_w, ada_b)

    w_in_b = w_in.astype(BF16)
    w_q = _permute_diff_cols(w_in_b[..., :seg], n_diff)
    w_k = _permute_diff_cols(w_in_b[..., seg:2 * seg], n_diff)
    w_out_b = w_out.astype(BF16)
    qkw = _qk_weight_table(diff_q_norm, diff_k_norm, na_q_norm, na_k_norm)
    cos_t, sin_t = _rope_tables(ctx_len, seq)
    na_slabs, na_slab_idx = _na_bias_slabs(na_rpb, n_tiles)

    lam_init = jnp.asarray([0.8 - 0.6 * math.exp(-0.3 * l) for l in range(depth)], F32)
    lam = (jnp.exp(jnp.sum(diff_lq1 * diff_lk1, axis=-1)) - jnp.exp(jnp.sum(diff_lq2 * diff_lk2, axis=-1))
           + lam_init)
    bound = (HEAD_DIM // 2) * jnp.max(jnp.abs(qkw[:, 0]), axis=-1) * jnp.max(jnp.abs(qkw[:, 1]), axis=-1)
    bound = bound * 1.02 + 0.01
    bound = jnp.where(bound <= MAX_STATIC_SHIFT, bound, -1.0)
    lam_tab = jnp.stack([lam, 1.0 - lam_init, bound], axis=1).reshape(DIFF_TAB * depth)

    wr = jnp.concatenate([moe_w_expert, moe_w_group,
                          jnp.zeros((depth, d, HEAD_DIM - N_EXPERTS - N_GROUPS), F32)], axis=-1)
    wr_hi = wr.astype(BF16)
    wr_lo = (wr - wr_hi.astype(F32)).astype(BF16)
    wr_split = jnp.stack([wr_hi, wr_lo], axis=1)

    h = _norm_mod(jnp.zeros((1,), jnp.int32), xa, norm1_w, mods)
    for l in range(depth):
        larr = jnp.full((1,), l, jnp.int32)
        proj = _in_proj(larr, h, w_q, w_k, w_in_b, qkw, cos_t, sin_t)
        d_out = _diff_attn(larr, lam_tab, proj, diff_subln, n_diff, ctx_len)
        n_out = _na_attn(larr, proj, na_slabs, na_slab_idx, n_na, n_diff)
        x_new, h2, ids, gates = _out_router(larr, d_out, n_out, w_out_b, xa, mods, norm2_w, wr_split)
        dest, slot_tok, block_expert, block_rows = _dispatch(ids[:, :2])
        y_slots = _experts(larr, block_expert, slot_tok, block_rows, h2, moe_w1, moe_w3, moe_w2)
        xa, h = _combine(larr, dest, y_slots, x_new, gates, mods, norm1_w)
    return xa[ctx_len:][None]
```
